```python
import jax, jax.numpy as jnp
from jax import lax
import numpy as np

D_MODEL = 1024
BATCH = 4
SEQ = 4096
DEPTH = 2

HEAD_DIM = 64
GRID_W = 64
NA_HEADS = D_MODEL // (2 * HEAD_DIM)
NA_KH_MAX = 8
NA_KW = 16
NA_WIDTH = NA_HEADS * HEAD_DIM
GQA_Q_HEADS = D_MODEL // (2 * HEAD_DIM)
GQA_KV_HEADS = GQA_Q_HEADS // 4
GQA_Q_WIDTH = GQA_Q_HEADS * HEAD_DIM
GQA_KV_WIDTH = GQA_KV_HEADS * HEAD_DIM
GQA_BLOCK = 128
ROPE_THETA = 10000.0
ROPE_ROW_DIMS = HEAD_DIM // 2
ROPE_COL_DIMS = HEAD_DIM // 2
IN_WIDTH = 3 * NA_WIDTH + GQA_Q_WIDTH + 2 * GQA_KV_WIDTH
N_BRANCH = 2
D_FF = ((8 * D_MODEL // 3 + 127) // 128) * 128
CONV_W = 3
PLE_DIM = 256
EPS = 1e-6

kernel_name = "hybrid_natten_gqa_gated_encoder"


def rmsnorm(x, g):
    xf = x.astype(jnp.float32)
    y = xf * lax.rsqrt(jnp.mean(xf * xf, axis=-1, keepdims=True) + EPS)
    return (y * g.astype(jnp.float32)).astype(x.dtype)


def axial_rope_tables(seq):
    t = jnp.arange(seq)
    row = (t // GRID_W).astype(jnp.float32)
    col = (t % GRID_W).astype(jnp.float32)

    def freqs(n):
        return 1.0 / (ROPE_THETA ** (jnp.arange(0, n, 2, dtype=jnp.float32) / n))

    ang = jnp.concatenate([row[:, None] * freqs(ROPE_ROW_DIMS)[None, :],
                           col[:, None] * freqs(ROPE_COL_DIMS)[None, :]], axis=-1)
    return jnp.cos(ang), jnp.sin(ang)


def apply_rope(x, cos, sin):
    b, s, h, dh = x.shape
    xf = x.astype(jnp.float32).reshape(b, s, h, dh // 2, 2)
    x0, x1 = xf[..., 0], xf[..., 1]
    c = cos[None, :, None, :]
    sn = sin[None, :, None, :]
    out = jnp.stack([x0 * c - x1 * sn, x0 * sn + x1 * c], axis=-1)
    return out.reshape(b, s, h, dh).astype(x.dtype)


def neighbourhood_attention(q, k, v, rpb):
    b, s, h, dh = q.shape
    rows = s // GRID_W
    kh = min(NA_KH_MAX, rows)
    qg = q.reshape(b, rows, GRID_W, h, dh)
    kg = k.reshape(b, rows, GRID_W, h, dh)
    vg = v.reshape(b, rows, GRID_W, h, dh)
    col = jnp.arange(GRID_W)
    col_start = jnp.clip(col - NA_KW // 2, 0, GRID_W - NA_KW)
    col_idx = col_start[:, None] + jnp.arange(NA_KW)[None, :]
    col_off = col_idx - col[:, None] + (NA_KW - 1)
    scale = dh ** -0.5

    def one_row(r):
        row_start = jnp.clip(r - kh // 2, 0, rows - kh)
        k_rows = lax.dynamic_slice_in_dim(kg, row_start, kh, axis=1)
        v_rows = lax.dynamic_slice_in_dim(vg, row_start, kh, axis=1)
        k_nb = k_rows[:, :, col_idx]
        v_nb = v_rows[:, :, col_idx]
        q_r = lax.dynamic_index_in_dim(qg, r, axis=1, keepdims=False)
        sc = jnp.einsum('bchd,bicjhd->bhcij', q_r, k_nb,
                        preferred_element_type=jnp.float32) * scale
        row_off = row_start + jnp.arange(kh) - r + (NA_KH_MAX - 1)
        bias = rpb[:, row_off][:, :, col_off]
        sc = sc + jnp.transpose(bias, (0, 2, 1, 3)).astype(jnp.float32)[None]
        pr = jax.nn.softmax(sc.reshape(b, h, GRID_W, kh * NA_KW), axis=-1)
        pr = pr.reshape(b, h, GRID_W, kh, NA_KW).astype(v.dtype)
        return jnp.einsum('bhcij,bicjhd->bchd', pr, v_nb)

    out = lax.map(one_row, jnp.arange(rows))
    return jnp.transpose(out, (1, 0, 2, 3, 4)).reshape(b, s, h * dh)


def gqa_blocked(q, k, v):
    b, s, hq, dh = q.shape
    g = hq // GQA_KV_HEADS
    nblk = s // GQA_BLOCK
    scale = dh ** -0.5
    qb = q.reshape(b, nblk, GQA_BLOCK, GQA_KV_HEADS, g, dh).transpose(1, 0, 2, 3, 4, 5)

    def one_block(qblk):
        sc = jnp.einsum('bqkgd,bskd->bkgqs', qblk, k,
                        preferred_element_type=jnp.float32) * scale
        pr = jax.nn.softmax(sc, axis=-1).astype(v.dtype)
        return jnp.einsum('bkgqs,bskd->bqkgd', pr, v)

    out = lax.map(one_block, qb)
    return out.transpose(1, 0, 2, 3, 4, 5).reshape(b, s, hq * dh)


def depthwise_conv_centred(u, w, bias):
    c = u.shape[-1]
    y = lax.conv_general_dilated(u, w[:, None, :].astype(u.dtype), window_strides=(1,),
                                 padding=[(CONV_W // 2, CONV_W // 2)],
                                 dimension_numbers=('NWC', 'WIO', 'NWC'),
                                 feature_group_count=c)
    return y + bias


def hybrid_layer(x, p_i, g_mix, w_in, qn_a, kn_a, rpb, qn_b, kn_b, w_gate, b_gate,
                 w_br_a, w_br_b, w_out, g_ffn, w_up, conv_w, conv_b, w_down,
                 w_ple, g_ple, w_ple_gate, cos, sin):
    b, s, d = x.shape
    h = rmsnorm(x, g_mix)
    proj = h @ w_in
    o1 = NA_WIDTH
    o2 = o1 + NA_WIDTH
    o3 = o2 + NA_WIDTH
    o4 = o3 + GQA_Q_WIDTH
    o5 = o4 + GQA_KV_WIDTH
    qa, ka, va, qb, kb, vb = jnp.split(proj, [o1, o2, o3, o4, o5], axis=-1)
    qa = rmsnorm(qa.reshape(b, s, NA_HEADS, HEAD_DIM), qn_a)
    ka = rmsnorm(ka.reshape(b, s, NA_HEADS, HEAD_DIM), kn_a)
    va = va.reshape(b, s, NA_HEADS, HEAD_DIM)
    out_a = neighbourhood_attention(qa, ka, va, rpb)
    qb = apply_rope(rmsnorm(qb.reshape(b, s, GQA_Q_HEADS, HEAD_DIM), qn_b), cos, sin)
    kb = apply_rope(rmsnorm(kb.reshape(b, s, GQA_KV_HEADS, HEAD_DIM), kn_b), cos, sin)
    vb = vb.reshape(b, s, GQA_KV_HEADS, HEAD_DIM)
    out_b = gqa_blocked(qb, kb, vb)
    gates = jax.nn.sigmoid(h @ w_gate + b_gate)
    gate_a, gate_b = jnp.split(gates, 2, axis=-1)
    merged = gate_a * (out_a @ w_br_a) + gate_b * (out_b @ w_br_b)
    x = x + merged @ w_out
    h2 = rmsnorm(x, g_ffn)
    u = depthwise_conv_centred(h2 @ w_up, conv_w, conv_b)
    u_gate, u_val = jnp.split(u, 2, axis=-1)
    x = x + (jax.nn.silu(u_gate) * u_val) @ w_down
    e = p_i @ w_ple
    x = x + jax.nn.sigmoid(rmsnorm(x, g_ple) @ w_ple_gate) * e
    return x


def setup_inputs(seed: int = 0) -> dict:
    key = jax.random.key(seed)
    ks = jax.random.split(key, 24)
    L, D = DEPTH, D_MODEL

    def nrm(k, shape, scale):
        return jax.random.normal(k, shape, jnp.float32) * scale

    def gain(k, shape):
        return 1.0 + 0.01 * jax.random.normal(k, shape, jnp.float32)

    return {
        "x": nrm(ks[0], (BATCH, SEQ, D), 1.0),
        "p": nrm(ks[1], (DEPTH, BATCH, SEQ, PLE_DIM), 1.0),
        "g_mix": gain(ks[2], (L, D)),
        "w_in": nrm(ks[3], (L, D, IN_WIDTH), D ** -0.5),
        "qn_a": gain(ks[4], (L, HEAD_DIM)),
        "kn_a": gain(ks[5], (L, HEAD_DIM)),
        "rpb": nrm(ks[6], (L, NA_HEADS, 2 * NA_KH_MAX - 1, 2 * NA_KW - 1), 0.02),
        "qn_b": gain(ks[7], (L, HEAD_DIM)),
        "kn_b": gain(ks[8], (L, HEAD_DIM)),
        "w_gate": nrm(ks[9], (L, D, N_BRANCH * D), D ** -0.5),
        "b_gate": nrm(ks[10], (L, N_BRANCH * D), 0.01),
        "w_br_a": nrm(ks[11], (L, NA_WIDTH, D), NA_WIDTH ** -0.5),
        "w_br_b": nrm(ks[12], (L, GQA_Q_WIDTH, D), GQA_Q_WIDTH ** -0.5),
        "w_out": nrm(ks[13], (L, D, D), D ** -0.5),
        "g_ffn": gain(ks[14], (L, D)),
        "w_up": nrm(ks[15], (L, D, 2 * D_FF), D ** -0.5),
        "conv_w": nrm(ks[16], (L, CONV_W, 2 * D_FF), CONV_W ** -0.5),
        "conv_b": nrm(ks[17], (L, 2 * D_FF), 0.01),
        "w_down": nrm(ks[18], (L, D_FF, D), D_FF ** -0.5),
        "w_ple": nrm(ks[19], (L, PLE_DIM, D), PLE_DIM ** -0.5),
        "g_ple": gain(ks[20], (L, D)),
        "w_ple_gate": nrm(ks[21], (L, D, D), D ** -0.5),
    }


def reference(x, p, g_mix, w_in, qn_a, kn_a, rpb, qn_b, kn_b, w_gate, b_gate,
              w_br_a, w_br_b, w_out, g_ffn, w_up, conv_w, conv_b, w_down,
              w_ple, g_ple, w_ple_gate):
    cos, sin = axial_rope_tables(x.shape[1])
    for i in range(DEPTH):
        x = hybrid_layer(x, p[i], g_mix[i], w_in[i], qn_a[i], kn_a[i], rpb[i], qn_b[i], kn_b[i],
                         w_gate[i], b_gate[i], w_br_a[i], w_br_b[i], w_out[i], g_ffn[i],
                         w_up[i], conv_w[i], conv_b[i], w_down[i], w_ple[i], g_ple[i],
                         w_ple_gate[i], cos, sin)
    return x
```

```python
import functools

import jax
import jax.numpy as jnp
from jax import lax
from jax.experimental import pallas as pl
from jax.experimental.pallas import tpu as pltpu

F32 = jnp.float32
BF16 = jnp.bfloat16

HEAD_DIM = 64
GRID_W = 64
NA_KH = 8
NA_KW = 16
GQA_GROUP = 4
ROPE_THETA = 10000.0
EPS = 1e-6

LANES = 128
SUBLANES = 8
MXU_DIM = 256
HEADS_PER_SLAB = LANES // HEAD_DIM

NA_ROWS_PER_STEP = 4
NA_KEY_ROWS = 12
MASK_VALUE = -1e30

ROW_TILE = 512
GQA_Q_TILE = 256
VMEM_LIMIT = 56 * 1024 * 1024


def _rms_rows(x, g):
    ms = jnp.mean(x * x, axis=-1, keepdims=True)
    return x * lax.rsqrt(ms + EPS) * g


def _dot(a, b):
    return jnp.dot(a, b, preferred_element_type=F32)


def _dot_nt(a, b):
    return lax.dot_general(a, b, (((1,), (1,)), ((), ())), preferred_element_type=F32)


def _sigmoid(z):
    return 1.0 / (1.0 + jnp.exp(-z))


def _low_half_mask():
    return lax.broadcasted_iota(jnp.int32, (1, LANES), 1) < HEAD_DIM


def _proj_kernel(x_ref, g_ref, w_ref, hm_ref, gn_ref, cos_ref, sin_ref,
                 qa_ref, ka_ref, qb_ref, kb_ref, va_ref, vb_ref, *, n_norm, n_rope_start):
    h = _rms_rows(x_ref[...], g_ref[...]).astype(BF16)
    even_lane = (lax.broadcasted_iota(jnp.int32, (1, LANES), 1) % 2) == 0
    outs = ((qa_ref, 2), (ka_ref, 2), (qb_ref, 2), (kb_ref, 1), (va_ref, 2), (vb_ref, 1))
    slab = 0
    for o_ref, n_slabs in outs:
        for t in range(n_slabs):
            c0 = slab * MXU_DIM
            y = _dot(h, w_ref[:, c0:c0 + MXU_DIM])
            if slab < n_norm:
                ms = _dot((y * y).astype(BF16), hm_ref[...])
                y = y * lax.rsqrt(ms + EPS) * gn_ref[:, c0:c0 + MXU_DIM]
                if slab >= n_rope_start:
                    halves = []
                    for u in range(MXU_DIM // LANES):
                        yh = y[:, u * LANES:(u + 1) * LANES]
                        sw = jnp.where(even_lane, pltpu.roll(yh, LANES - 1, 1), pltpu.roll(yh, 1, 1))
                        halves.append(yh * cos_ref[...] + sw * sin_ref[...])
                    y = jnp.concatenate(halves, axis=1)
            o_ref[:, t * MXU_DIM:(t + 1) * MXU_DIM] = y.astype(o_ref.dtype)
            slab += 1


def _proj_call(x2, g_mix, w_cat, head_mean, gains, cos_t, sin_t, seq):
    rows, d = x2.shape
    tm = ROW_TILE
    n_cols = w_cat.shape[1]
    tiles_per_seq = seq // tm
    widths = (512, 512, 512, 256, 512, 256)
    row_spec = lambda w: pl.BlockSpec((tm, w), lambda i: (i, 0))
    const = lambda shape: pl.BlockSpec(shape, lambda i: (0, 0))
    kern = functools.partial(_proj_kernel, n_norm=7, n_rope_start=4)
    return pl.pallas_call(
        kern,
        grid=(rows // tm,),
        in_specs=[
            row_spec(d),
            const((1, d)),
            const((d, n_cols)),
            const((MXU_DIM, MXU_DIM)),
            const((1, gains.shape[1])),
            pl.BlockSpec((tm, LANES), lambda i: (i % tiles_per_seq, 0)),
            pl.BlockSpec((tm, LANES), lambda i: (i % tiles_per_seq, 0)),
        ],
        out_specs=[row_spec(w) for w in widths],
        out_shape=[jax.ShapeDtypeStruct((rows, w), BF16) for w in widths],
        compiler_params=pltpu.CompilerParams(dimension_semantics=("parallel",), vmem_limit_bytes=VMEM_LIMIT),
        name="proj",
    )(x2, g_mix, w_cat, head_mean, gains, cos_t, sin_t)


def _natten_kernel(q_ref, k_ref, v_ref, bias_ref, o_ref, *, n_groups, grid_rows):
    low = _low_half_mask()
    q_rows = NA_ROWS_PER_STEP * GRID_W
    k_rows = NA_KEY_ROWS * GRID_W

    def body(g, carry):
        r0 = g * NA_ROWS_PER_STEP
        k0 = jnp.clip(r0 - NA_KH // 2, 0, grid_rows - NA_KEY_ROWS)
        qs = pl.multiple_of(r0 * GRID_W, GRID_W)
        ks = pl.multiple_of(k0 * GRID_W, GRID_W)
        q2 = q_ref[pl.ds(qs, q_rows), :]
        k2 = k_ref[pl.ds(ks, k_rows), :]
        v2 = v_ref[pl.ds(ks, k_rows), :]
        var = jnp.where(g == 0, 0, jnp.where(g == n_groups - 1, 2, 1))
        outs = []
        for par in range(HEADS_PER_SLAB):
            sel = low if par == 0 else jnp.logical_not(low)
            qm = jnp.where(sel, q2, jnp.zeros_like(q2))
            s = _dot_nt(qm, k2) + bias_ref[par, var]
            m = jnp.max(s, axis=-1, keepdims=True)
            p = jnp.exp(s - m)
            l = jnp.sum(p, axis=-1, keepdims=True)
            outs.append(_dot(p.astype(BF16), v2) * (1.0 / l))
        o_ref[pl.ds(qs, q_rows), :] = jnp.where(low, outs[0], outs[1]).astype(o_ref.dtype)
        return carry

    lax.fori_loop(0, n_groups, body, 0)


def _natten_call(qa, ka, va, bias, batch, seq):
    rows, width = qa.shape
    n_slabs = width // LANES
    grid_rows = seq // GRID_W
    n_groups = grid_rows // NA_ROWS_PER_STEP
    slab_spec = pl.BlockSpec((seq, LANES), lambda j, b: (b, j))
    kern = functools.partial(_natten_kernel, n_groups=n_groups, grid_rows=grid_rows)
    return pl.pallas_call(
        kern,
        grid=(n_slabs, batch),
        in_specs=[slab_spec, slab_spec, slab_spec,
                  pl.BlockSpec((HEADS_PER_SLAB,) + bias.shape[1:], lambda j, b: (j, 0, 0, 0))],
        out_specs=slab_spec,
        out_shape=jax.ShapeDtypeStruct((rows, width), BF16),
        compiler_params=pltpu.CompilerParams(dimension_semantics=("parallel", "parallel"),
                                             vmem_limit_bytes=VMEM_LIMIT),
        name="natten",
    )(qa, ka, va, bias)


def _natten_bias_table(rpb):
    grid_rows = GRID_W
    j = jnp.arange(NA_ROWS_PER_STEP)
    rel = jnp.arange(NA_KEY_ROWS)
    c = jnp.arange(GRID_W)
    cs = jnp.clip(c - NA_KW // 2, 0, GRID_W - NA_KW)
    col_valid = (c[None, :] >= cs[:, None]) & (c[None, :] < cs[:, None] + NA_KW)
    col_idx = jnp.clip(c[None, :] - c[:, None] + NA_KW - 1, 0, 2 * NA_KW - 2)
    tables = []
    for r0 in (0, NA_ROWS_PER_STEP, grid_rows - NA_ROWS_PER_STEP):
        k0 = min(max(r0 - NA_KH // 2, 0), grid_rows - NA_KEY_ROWS)
        r = r0 + j
        kr = k0 + rel
        rs = jnp.clip(r - NA_KH // 2, 0, grid_rows - NA_KH)
        row_valid = (kr[None, :] >= rs[:, None]) & (kr[None, :] < rs[:, None] + NA_KH)
        row_idx = jnp.clip(kr[None, :] - r[:, None] + NA_KH - 1, 0, 2 * NA_KH - 2)
        b = rpb[:, row_idx[:, None, :, None], col_idx[None, :, None, :]]
        valid = row_valid[:, None, :, None] & col_valid[None, :, None, :]
        b = jnp.where(valid[None], b.astype(F32), MASK_VALUE)
        tables.append(b.reshape(rpb.shape[0], NA_ROWS_PER_STEP * GRID_W, NA_KEY_ROWS * GRID_W))
    return jnp.stack(tables, axis=1)


def _gqa_kernel(q_ref, k_ref, v_ref, o_ref):
    low = _low_half_mask()
    k = k_ref[...]
    v = v_ref[...]
    for j in range(q_ref.shape[1] // LANES):
        q2 = q_ref[:, j * LANES:(j + 1) * LANES]
        outs = []
        for par in range(HEADS_PER_SLAB):
            sel = low if par == 0 else jnp.logical_not(low)
            qm = jnp.where(sel, q2, jnp.zeros_like(q2))
            s = _dot_nt(qm, k)
            m = jnp.max(s, axis=-1, keepdims=True)
            p = jnp.exp(s - m)
            l = jnp.sum(p, axis=-1, keepdims=True)
            outs.append(_dot(p.astype(BF16), v) * (1.0 / l))
        o_ref[:, j * LANES:(j + 1) * LANES] = jnp.where(low, outs[0], outs[1]).astype(o_ref.dtype)


def _gqa_call(qb, kb, vb, batch, seq):
    rows, width = qb.shape
    n_kv = kb.shape[1] // LANES
    tq = GQA_Q_TILE
    q_width = width // n_kv
    blocks = seq // tq
    q_spec = pl.BlockSpec((tq, q_width), lambda b, h, i: (b * blocks + i, h))
    kv_spec = pl.BlockSpec((seq, LANES), lambda b, h, i: (b, h))
    return pl.pallas_call(
        _gqa_kernel,
        grid=(batch, n_kv, blocks),
        in_specs=[q_spec, kv_spec, kv_spec],
        out_specs=q_spec,
        out_shape=jax.ShapeDtypeStruct((rows, width), BF16),
        compiler_params=pltpu.CompilerParams(dimension_semantics=("parallel", "parallel", "parallel"),
                                             vmem_limit_bytes=VMEM_LIMIT),
        name="gqa",
    )(qb, kb, vb)


def _merge_kernel(x_ref, g_ref, oa_ref, ob_ref, wg_ref, bg_ref, wa_ref, wb_ref, wo_ref, o_ref):
    x = x_ref[...]
    d = x.shape[1]
    h = _rms_rows(x, g_ref[...]).astype(BF16)
    gates = _sigmoid(_dot(h, wg_ref[...]) + bg_ref[...])
    merged = gates[:, :d] * _dot(oa_ref[...], wa_ref[...]) + gates[:, d:] * _dot(ob_ref[...], wb_ref[...])
    o_ref[...] = x + _dot(merged.astype(BF16), wo_ref[...])


def _merge_call(x2, g_mix, out_a, out_b, w_gate, b_gate, w_br_a, w_br_b, w_out):
    rows, d = x2.shape
    tm = ROW_TILE
    row_spec = lambda w: pl.BlockSpec((tm, w), lambda i: (i, 0))
    const = lambda a: pl.BlockSpec(a.shape, lambda i: (0, 0))
    return pl.pallas_call(
        _merge_kernel,
        grid=(rows // tm,),
        in_specs=[row_spec(d), const(g_mix), row_spec(out_a.shape[1]), row_spec(out_b.shape[1]),
                  const(w_gate), const(b_gate), const(w_br_a), const(w_br_b), const(w_out)],
        out_specs=row_spec(d),
        out_shape=jax.ShapeDtypeStruct((rows, d), F32),
        compiler_params=pltpu.CompilerParams(dimension_semantics=("parallel",), vmem_limit_bytes=VMEM_LIMIT),
        name="merge",
    )(x2, g_mix, out_a, out_b, w_gate, b_gate, w_br_a, w_br_b, w_out)


def _ffn_kernel(x_ref, xp_ref, xn_ref, p_ref, gf_ref, wu_ref, cw_ref, cb_ref, wd_ref, wp_ref, gp_ref, wpg_ref,
                o_ref, ug_ref, uv_ref, *, tiles_per_seq, n_chunks):
    i = pl.program_id(0)
    tm, d = x_ref.shape
    d_ff = wd_ref.shape[0]
    fc = d_ff // n_chunks
    x = x_ref[...]
    gf = gf_ref[...]
    h = _rms_rows(x, gf).astype(BF16)
    prev_ok = ((i % tiles_per_seq) != 0).astype(F32)
    next_ok = (((i + 1) % tiles_per_seq) != 0).astype(F32)
    h_halo = _rms_rows(jnp.concatenate([xp_ref[...], xn_ref[...]], axis=0), gf).astype(BF16)

    def conv_part(u_ref, c0):
        w = wu_ref[:, c0:c0 + fc]
        u = _dot(h, w)
        uh = _dot(h_halo, w)
        u_ref[0:SUBLANES, :] = uh[0:SUBLANES] * prev_ok
        u_ref[SUBLANES:SUBLANES + tm, :] = u
        u_ref[SUBLANES + tm:2 * SUBLANES + tm, :] = uh[SUBLANES:] * next_ok
        cw = cw_ref[:, c0:c0 + fc]
        return (cw[0:1] * u_ref[SUBLANES - 1:SUBLANES - 1 + tm, :] + cw[1:2] * u
                + cw[2:3] * u_ref[SUBLANES + 1:SUBLANES + 1 + tm, :] + cb_ref[:, c0:c0 + fc])

    acc = jnp.zeros((tm, d), F32)
    for c in range(n_chunks):
        yg = conv_part(ug_ref, c * fc)
        yv = conv_part(uv_ref, d_ff + c * fc)
        act = (yg * _sigmoid(yg) * yv).astype(BF16)
        acc = acc + _dot(act, wd_ref[c * fc:(c + 1) * fc, :])
    x2 = x + acc
    e = _dot(p_ref[...].astype(BF16), wp_ref[...])
    gate = _sigmoid(_dot(_rms_rows(x2, gp_ref[...]).astype(BF16), wpg_ref[...]))
    o_ref[...] = x2 + gate * e


def _ffn_call(x2, p2, g_ffn, w_up, conv_w, conv_b, w_down, w_ple, g_ple, w_ple_gate, seq):
    rows, d = x2.shape
    tm = ROW_TILE
    d_ff = w_down.shape[0]
    n_chunks = 2
    fc = d_ff // n_chunks
    tiles_per_seq = seq // tm
    blocks8 = tm // SUBLANES
    last8 = rows // SUBLANES - 1
    row_spec = lambda w: pl.BlockSpec((tm, w), lambda i: (i, 0))
    const = lambda a: pl.BlockSpec(a.shape, lambda i: (0, 0), pipeline_mode=pl.Buffered(1))
    kern = functools.partial(_ffn_kernel, tiles_per_seq=tiles_per_seq, n_chunks=n_chunks)
    return pl.pallas_call(
        kern,
        grid=(rows // tm,),
        in_specs=[
            row_spec(d),
            pl.BlockSpec((SUBLANES, d), lambda i: (jnp.maximum(i * blocks8 - 1, 0), 0)),
            pl.BlockSpec((SUBLANES, d), lambda i: (jnp.minimum((i + 1) * blocks8, last8), 0)),
            row_spec(p2.shape[1]),
            const(g_ffn), const(w_up), const(conv_w), const(conv_b), const(w_down),
            const(w_ple), const(g_ple), const(w_ple_gate),
        ],
        out_specs=row_spec(d),
        out_shape=jax.ShapeDtypeStruct((rows, d), F32),
        scratch_shapes=[pltpu.VMEM((tm + 2 * SUBLANES, fc), F32), pltpu.VMEM((tm + 2 * SUBLANES, fc), F32)],
        compiler_params=pltpu.CompilerParams(dimension_semantics=("parallel",), vmem_limit_bytes=VMEM_LIMIT),
        name="ffn",
    )(x2, x2, x2, p2, g_ffn, w_up, conv_w, conv_b, w_down, w_ple, g_ple, w_ple_gate)


def _rope_tables(seq):
    t = jnp.arange(seq)
    row = (t // GRID_W).astype(F32)
    col = (t % GRID_W).astype(F32)
    n = HEAD_DIM // 2
    freqs = 1.0 / (ROPE_THETA ** (jnp.arange(0, n, 2, dtype=F32) / n))
    ang = jnp.concatenate([row[:, None] * freqs[None, :], col[:, None] * freqs[None, :]], axis=-1)
    cos = jnp.repeat(jnp.cos(ang), 2, axis=-1)
    sin = jnp.repeat(jnp.sin(ang), 2, axis=-1)
    sign = jnp.where(jnp.arange(HEAD_DIM) % 2 == 0, -1.0, 1.0).astype(F32)
    reps = LANES // HEAD_DIM
    return jnp.tile(cos, (1, reps)), jnp.tile(sin * sign[None, :], (1, reps))


def _head_mean_matrix():
    idx = jnp.arange(MXU_DIM) // HEAD_DIM
    return jnp.where(idx[:, None] == idx[None, :], 1.0 / HEAD_DIM, 0.0).astype(BF16)


def _dup_heads(w, n_heads):
    d = w.shape[0]
    w = w.reshape(d, n_heads, 1, HEAD_DIM)
    return jnp.broadcast_to(w, (d, n_heads, HEADS_PER_SLAB, HEAD_DIM)).reshape(d, n_heads * LANES)


def kernel(x, p, g_mix, w_in, qn_a, kn_a, rpb, qn_b, kn_b, w_gate, b_gate, w_br_a, w_br_b, w_out, g_ffn, w_up,
           conv_w, conv_b, w_down, w_ple, g_ple, w_ple_gate):
    batch, seq, d = x.shape
    depth = p.shape[0]
    rows = batch * seq
    na_w = w_br_a.shape[1]
    q_w = w_br_b.shape[1]
    n_kv = q_w // HEAD_DIM // GQA_GROUP
    kv_w = n_kv * HEAD_DIM
    assert seq == GRID_W * GRID_W and seq % ROW_TILE == 0 and rows % ROW_TILE == 0
    assert w_in.shape[2] == 3 * na_w + q_w + 2 * kv_w

    cos_t, sin_t = _rope_tables(seq)
    head_mean = _head_mean_matrix()
    q_scale = HEAD_DIM ** -0.5
    x2 = x.reshape(rows, d)

    for i in range(depth):
        o1, o2, o3 = na_w, 2 * na_w, 3 * na_w
        o4, o5 = o3 + q_w, o3 + q_w + kv_w
        wi = w_in[i]
        w_cat = jnp.concatenate(
            [wi[:, :o1], wi[:, o1:o2], wi[:, o3:o4], _dup_heads(wi[:, o4:o5], n_kv),
             wi[:, o2:o3], _dup_heads(wi[:, o5:], n_kv)], axis=1).astype(BF16)
        gains = jnp.concatenate(
            [jnp.tile(qn_a[i] * q_scale, na_w // HEAD_DIM), jnp.tile(kn_a[i], na_w // HEAD_DIM),
             jnp.tile(qn_b[i] * q_scale, q_w // HEAD_DIM), jnp.tile(kn_b[i], n_kv * HEADS_PER_SLAB)])[None, :]
        qa, ka, qb, kb, va, vb = _proj_call(x2, g_mix[i][None, :], w_cat, head_mean, gains.astype(F32),
                                            cos_t, sin_t, seq)
        out_a = _natten_call(qa, ka, va, _natten_bias_table(rpb[i]), batch, seq)
        out_b = _gqa_call(qb, kb, vb, batch, seq)
        x2 = _merge_call(x2, g_mix[i][None, :], out_a, out_b, w_gate[i].astype(BF16), b_gate[i][None, :],
                         w_br_a[i].astype(BF16), w_br_b[i].astype(BF16), w_out[i].astype(BF16))
        x2 = _ffn_call(x2, p[i].reshape(rows, -1), g_ffn[i][None, :], w_up[i].astype(BF16), conv_w[i],
                       conv_b[i][None, :], w_down[i].astype(BF16), w_ple[i].astype(BF16), g_ple[i][None, :],
                       w_ple_gate[i].astype(BF16), seq)
    return x2.reshape(batch, seq, d)
```

```python
import functools

import jax
import jax.numpy as jnp
from jax import lax
from jax.experimental import pallas as pl
from jax.experimental.pallas import tpu as pltpu

F32 = jnp.float32
BF16 = jnp.bfloat16

HEAD_DIM = 64
GRID_W = 64
NA_KH = 8
NA_KW = 16
GQA_GROUP = 4
ROPE_THETA = 10000.0
EPS = 1e-6

LANES = 128
SUBLANES = 8
MXU_DIM = 256
HEADS_PER_SLAB = LANES // HEAD_DIM

NA_ROWS_PER_STEP = 4
NA_KEY_ROWS = 12
MASK_VALUE = -1e30

ROW_TILE = 512
GQA_Q_TILE = 256
VMEM_LIMIT = 56 * 1024 * 1024


def _rms_rows(x, g):
    ms = jnp.mean(x * x, axis=-1, keepdims=True)
    return x * lax.rsqrt(ms + EPS) * g


def _dot(a, b):
    return jnp.dot(a, b, preferred_element_type=F32)


def _dot_nt(a, b):
    return lax.dot_general(a, b, (((1,), (1,)), ((), ())), preferred_element_type=F32)


def _sigmoid(z):
    return 1.0 / (1.0 + jnp.exp(-z))


def _low_half_mask():
    return lax.broadcasted_iota(jnp.int32, (1, LANES), 1) < HEAD_DIM


def _proj_kernel(x_ref, g_ref, w_ref, hm_ref, gn_ref, cos_ref, sin_ref,
                 qa_ref, ka_ref, qb_ref, kb_ref, va_ref, vb_ref, *, n_norm, n_rope_start):
    h = _rms_rows(x_ref[...], g_ref[...]).astype(BF16)
    even_lane = (lax.broadcasted_iota(jnp.int32, (1, LANES), 1) % 2) == 0
    outs = ((qa_ref, 2), (ka_ref, 2), (qb_ref, 2), (kb_ref, 1), (va_ref, 2), (vb_ref, 1))
    slab = 0
    for o_ref, n_slabs in outs:
        for t in range(n_slabs):
            c0 = slab * MXU_DIM
            y = _dot(h, w_ref[:, c0:c0 + MXU_DIM])
            if slab < n_norm:
                ms = _dot((y * y).astype(BF16), hm_ref[...])
                y = y * lax.rsqrt(ms + EPS) * gn_ref[:, c0:c0 + MXU_DIM]
                if slab >= n_rope_start:
                    halves = []
                    for u in range(MXU_DIM // LANES):
                        yh = y[:, u * LANES:(u + 1) * LANES]
                        sw = jnp.where(even_lane, pltpu.roll(yh, LANES - 1, 1), pltpu.roll(yh, 1, 1))
                        halves.append(yh * cos_ref[...] + sw * sin_ref[...])
                    y = jnp.concatenate(halves, axis=1)
            o_ref[:, t * MXU_DIM:(t + 1) * MXU_DIM] = y.astype(o_ref.dtype)
            slab += 1


def _proj_call(x2, g_mix, w_cat, head_mean, gains, cos_t, sin_t, seq):
    rows, d = x2.shape
    tm = ROW_TILE
    n_cols = w_cat.shape[1]
    tiles_per_seq = seq // tm
    widths = (512, 512, 512, 256, 512, 256)
    row_spec = lambda w: pl.BlockSpec((tm, w), lambda i: (i, 0))
    const = lambda shape: pl.BlockSpec(shape, lambda i: (0, 0))
    kern = functools.partial(_proj_kernel, n_norm=7, n_rope_start=4)
    return pl.pallas_call(
        kern,
        grid=(rows // tm,),
        in_specs=[
            row_spec(d),
            const((1, d)),
            const((d, n_cols)),
            const((MXU_DIM, MXU_DIM)),
            const((1, gains.shape[1])),
            pl.BlockSpec((tm, LANES), lambda i: (i % tiles_per_seq, 0)),
            pl.BlockSpec((tm, LANES), lambda i: (i % tiles_per_seq, 0)),
        ],
        out_specs=[row_spec(w) for w in widths],
        out_shape=[jax.ShapeDtypeStruct((rows, w), BF16) for w in widths],
        compiler_params=pltpu.CompilerParams(dimension_semantics=("parallel",), vmem_limit_bytes=VMEM_LIMIT),
        name="proj",
    )(x2, g_mix, w_cat, head_mean, gains, cos_t, sin_t)


def _natten_kernel(q_ref, k_ref, v_ref, bias_ref, o_ref, *, n_groups, grid_rows):
    low = _low_half_mask()
    q_rows = NA_ROWS_PER_STEP * GRID_W
    k_rows = NA_KEY_ROWS * GRID_W

    def body(g, carry):
        r0 = g * NA_ROWS_PER_STEP
        k0 = jnp.clip(r0 - NA_KH // 2, 0, grid_rows - NA_KEY_ROWS)
        qs = pl.multiple_of(r0 * GRID_W, GRID_W)
        ks = pl.multiple_of(k0 * GRID_W, GRID_W)
        q2 = q_ref[pl.ds(qs, q_rows), :]
        k2 = k_ref[pl.ds(ks, k_rows), :]
        v2 = v_ref[pl.ds(ks, k_rows), :]
        var = jnp.where(g == 0, 0, jnp.where(g == n_groups - 1, 2, 1))
        outs = []
        for par in range(HEADS_PER_SLAB):
            sel = low if par == 0 else jnp.logical_not(low)
            qm = jnp.where(sel, q2, jnp.zeros_like(q2))
            s = _dot_nt(qm, k2) + bias_ref[par, var]
            m = jnp.max(s, axis=-1, keepdims=True)
            p = jnp.exp(s - m)
            l = jnp.sum(p, axis=-1, keepdims=True)
            outs.append(_dot(p.astype(BF16), v2) * (1.0 / l))
        o_ref[pl.ds(qs, q_rows), :] = jnp.where(low, outs[0], outs[1]).astype(o_ref.dtype)
        return carry

    lax.fori_loop(0, n_groups, body, 0)


def _natten_call(qa, ka, va, bias, layer, batch, seq):
    rows, width = qa.shape
    n_slabs = width // LANES
    first = layer * n_slabs
    grid_rows = seq // GRID_W
    n_groups = grid_rows // NA_ROWS_PER_STEP
    slab_spec = pl.BlockSpec((seq, LANES), lambda j, b: (b, j))
    kern = functools.partial(_natten_kernel, n_groups=n_groups, grid_rows=grid_rows)
    return pl.pallas_call(
        kern,
        grid=(n_slabs, batch),
        in_specs=[slab_spec, slab_spec, slab_spec,
                  pl.BlockSpec((HEADS_PER_SLAB,) + bias.shape[1:], lambda j, b: (first + j, 0, 0, 0))],
        out_specs=slab_spec,
        out_shape=jax.ShapeDtypeStruct((rows, width), BF16),
        compiler_params=pltpu.CompilerParams(dimension_semantics=("parallel", "parallel"),
                                             vmem_limit_bytes=VMEM_LIMIT),
        name="natten",
    )(qa, ka, va, bias)


def _natten_bias_kernel(w_ref, o_ref, *, grid_rows):
    n_pairs = w_ref.shape[1]
    c = lax.broadcasted_iota(jnp.int32, (GRID_W, LANES), 0)
    lane = lax.broadcasted_iota(jnp.int32, (GRID_W, LANES), 1)
    kc = lane % GRID_W
    cs = jnp.clip(c - NA_KW // 2, 0, GRID_W - NA_KW)
    col_valid = (kc >= cs) & (kc < cs + NA_KW)
    upper = (lane >= GRID_W).astype(jnp.int32)
    masked = jnp.full((GRID_W, LANES), MASK_VALUE, F32)
    lo, hi = NA_KH // 2 - 1, NA_KH // 2 + NA_KH - 2
    pairs, pairs_interior = [], []
    for a in range(n_pairs):
        x = jnp.broadcast_to(w_ref[0, a:a + 1, :], (GRID_W, LANES))
        y = pltpu.roll(x, LANES - (NA_KW - 1), 1, stride=1, stride_axis=0)
        blk = jnp.where(col_valid, y, masked)
        row_idx = a + upper
        pairs.append(blk)
        pairs_interior.append(jnp.where((row_idx >= lo) & (row_idx <= hi), blk, masked))
    r0s = (0, NA_ROWS_PER_STEP, grid_rows - NA_ROWS_PER_STEP)
    for var, r0 in enumerate(r0s):
        k0 = min(max(r0 - NA_KH // 2, 0), grid_rows - NA_KEY_ROWS)
        rs = [min(max(r0 + j - NA_KH // 2, 0), grid_rows - NA_KH) for j in range(NA_ROWS_PER_STEP)]
        for j in range(NA_ROWS_PER_STEP):
            for t in range(NA_KEY_ROWS // 2):
                a = 2 * t - j + (k0 - r0) + NA_KH - 1
                rows_ok = [rs[j] <= k0 + rel < rs[j] + NA_KH for rel in (2 * t, 2 * t + 1)]
                if var == 1:
                    blk = pairs_interior[a]
                elif all(rows_ok):
                    blk = pairs[a]
                else:
                    assert not any(rows_ok)
                    blk = masked
                o_ref[0, var, j * GRID_W:(j + 1) * GRID_W, t * LANES:(t + 1) * LANES] = blk


def _natten_bias_call(rpb):
    n_l, n_h, n_r, n_c = rpb.shape
    assert n_r == 2 * NA_KH - 1 and n_c == 2 * NA_KW - 1 and n_c <= GRID_W
    rp = jnp.pad(rpb.astype(F32), ((0, 0), (0, 0), (0, 0), (0, GRID_W - n_c)))
    pairs = jnp.concatenate([rp[:, :, :-1], rp[:, :, 1:]], axis=-1).reshape(n_l * n_h, n_r - 1, LANES)
    q_rows, k_rows = NA_ROWS_PER_STEP * GRID_W, NA_KEY_ROWS * GRID_W
    kern = functools.partial(_natten_bias_kernel, grid_rows=GRID_W)
    return pl.pallas_call(
        kern,
        grid=(n_l * n_h,),
        in_specs=[pl.BlockSpec((1, n_r - 1, LANES), lambda h: (h, 0, 0))],
        out_specs=pl.BlockSpec((1, 3, q_rows, k_rows), lambda h: (h, 0, 0, 0)),
        out_shape=jax.ShapeDtypeStruct((n_l * n_h, 3, q_rows, k_rows), F32),
        compiler_params=pltpu.CompilerParams(dimension_semantics=("parallel",), vmem_limit_bytes=VMEM_LIMIT),
        name="natten_bias",
    )(pairs)


def _gqa_kernel(q_ref, k_ref, v_ref, o_ref):
    low = _low_half_mask()
    k = k_ref[...]
    v = v_ref[...]
    for j in range(q_ref.shape[1] // LANES):
        q2 = q_ref[:, j * LANES:(j + 1) * LANES]
        outs = []
        for par in range(HEADS_PER_SLAB):
            sel = low if par == 0 else jnp.logical_not(low)
            qm = jnp.where(sel, q2, jnp.zeros_like(q2))
            s = _dot_nt(qm, k)
            m = jnp.max(s, axis=-1, keepdims=True)
            p = jnp.exp(s - m)
            l = jnp.sum(p, axis=-1, keepdims=True)
            outs.append(_dot(p.astype(BF16), v) * (1.0 / l))
        o_ref[:, j * LANES:(j + 1) * LANES] = jnp.where(low, outs[0], outs[1]).astype(o_ref.dtype)


def _gqa_call(qb, kb, vb, batch, seq):
    rows, width = qb.shape
    n_kv = kb.shape[1] // LANES
    tq = GQA_Q_TILE
    q_width = width // n_kv
    blocks = seq // tq
    q_spec = pl.BlockSpec((tq, q_width), lambda b, h, i: (b * blocks + i, h))
    kv_spec = pl.BlockSpec((seq, LANES), lambda b, h, i: (b, h))
    return pl.pallas_call(
        _gqa_kernel,
        grid=(batch, n_kv, blocks),
        in_specs=[q_spec, kv_spec, kv_spec],
        out_specs=q_spec,
        out_shape=jax.ShapeDtypeStruct((rows, width), BF16),
        compiler_params=pltpu.CompilerParams(dimension_semantics=("parallel", "parallel", "parallel"),
                                             vmem_limit_bytes=VMEM_LIMIT),
        name="gqa",
    )(qb, kb, vb)


def _merge_kernel(x_ref, g_ref, oa_ref, ob_ref, wg_ref, bg_ref, wa_ref, wb_ref, wo_ref, o_ref):
    x = x_ref[...]
    d = x.shape[1]
    h = _rms_rows(x, g_ref[...]).astype(BF16)
    gates = _sigmoid(_dot(h, wg_ref[...]) + bg_ref[...])
    merged = gates[:, :d] * _dot(oa_ref[...], wa_ref[...]) + gates[:, d:] * _dot(ob_ref[...], wb_ref[...])
    o_ref[...] = x + _dot(merged.astype(BF16), wo_ref[...])


def _merge_call(x2, g_mix, out_a, out_b, w_gate, b_gate, w_br_a, w_br_b, w_out):
    rows, d = x2.shape
    tm = ROW_TILE
    row_spec = lambda w: pl.BlockSpec((tm, w), lambda i: (i, 0))
    const = lambda a: pl.BlockSpec(a.shape, lambda i: (0, 0))
    return pl.pallas_call(
        _merge_kernel,
        grid=(rows // tm,),
        in_specs=[row_spec(d), const(g_mix), row_spec(out_a.shape[1]), row_spec(out_b.shape[1]),
                  const(w_gate), const(b_gate), const(w_br_a), const(w_br_b), const(w_out)],
        out_specs=row_spec(d),
        out_shape=jax.ShapeDtypeStruct((rows, d), F32),
        compiler_params=pltpu.CompilerParams(dimension_semantics=("parallel",), vmem_limit_bytes=VMEM_LIMIT),
        name="merge",
    )(x2, g_mix, out_a, out_b, w_gate, b_gate, w_br_a, w_br_b, w_out)


def _ffn_kernel(x_ref, xp_ref, xn_ref, p_ref, gf_ref, wu_ref, cw_ref, cb_ref, wd_ref, wp_ref, gp_ref, wpg_ref,
                o_ref, ug_ref, uv_ref, *, tiles_per_seq, n_chunks):
    i = pl.program_id(0)
    tm, d = x_ref.shape
    d_ff = wd_ref.shape[0]
    fc = d_ff // n_chunks
    x = x_ref[...]
    gf = gf_ref[...]
    h = _rms_rows(x, gf).astype(BF16)
    prev_ok = ((i % tiles_per_seq) != 0).astype(F32)
    next_ok = (((i + 1) % tiles_per_seq) != 0).astype(F32)
    h_halo = _rms_rows(jnp.concatenate([xp_ref[...], xn_ref[...]], axis=0), gf).astype(BF16)

    def conv_part(u_ref, c0):
        w = wu_ref[:, c0:c0 + fc]
        u = _dot(h, w)
        uh = _dot(h_halo, w)
        u_ref[0:SUBLANES, :] = uh[0:SUBLANES] * prev_ok
        u_ref[SUBLANES:SUBLANES + tm, :] = u
        u_ref[SUBLANES + tm:2 * SUBLANES + tm, :] = uh[SUBLANES:] * next_ok
        cw = cw_ref[:, c0:c0 + fc]
        return (cw[0:1] * u_ref[SUBLANES - 1:SUBLANES - 1 + tm, :] + cw[1:2] * u
                + cw[2:3] * u_ref[SUBLANES + 1:SUBLANES + 1 + tm, :] + cb_ref[:, c0:c0 + fc])

    acc = jnp.zeros((tm, d), F32)
    for c in range(n_chunks):
        yg = conv_part(ug_ref, c * fc)
        yv = conv_part(uv_ref, d_ff + c * fc)
        act = (yg * _sigmoid(yg) * yv).astype(BF16)
        acc = acc + _dot(act, wd_ref[c * fc:(c + 1) * fc, :])
    x2 = x + acc
    e = _dot(p_ref[...].astype(BF16), wp_ref[...])
    gate = _sigmoid(_dot(_rms_rows(x2, gp_ref[...]).astype(BF16), wpg_ref[...]))
    o_ref[...] = x2 + gate * e


def _ffn_call(x2, p2, g_ffn, w_up, conv_w, conv_b, w_down, w_ple, g_ple, w_ple_gate, seq):
    rows, d = x2.shape
    tm = ROW_TILE
    d_ff = w_down.shape[0]
    n_chunks = 2
    fc = d_ff // n_chunks
    tiles_per_seq = seq // tm
    blocks8 = tm // SUBLANES
    last8 = rows // SUBLANES - 1
    row_spec = lambda w: pl.BlockSpec((tm, w), lambda i: (i, 0))
    const = lambda a: pl.BlockSpec(a.shape, lambda i: (0, 0), pipeline_mode=pl.Buffered(1))
    kern = functools.partial(_ffn_kernel, tiles_per_seq=tiles_per_seq, n_chunks=n_chunks)
    return pl.pallas_call(
        kern,
        grid=(rows // tm,),
        in_specs=[
            row_spec(d),
            pl.BlockSpec((SUBLANES, d), lambda i: (jnp.maximum(i * blocks8 - 1, 0), 0)),
            pl.BlockSpec((SUBLANES, d), lambda i: (jnp.minimum((i + 1) * blocks8, last8), 0)),
            row_spec(p2.shape[1]),
            const(g_ffn), const(w_up), const(conv_w), const(conv_b), const(w_down),
            const(w_ple), const(g_ple), const(w_ple_gate),
        ],
        out_specs=row_spec(d),
        out_shape=jax.ShapeDtypeStruct((rows, d), F32),
        scratch_shapes=[pltpu.VMEM((tm + 2 * SUBLANES, fc), F32), pltpu.VMEM((tm + 2 * SUBLANES, fc), F32)],
        compiler_params=pltpu.CompilerParams(dimension_semantics=("parallel",), vmem_limit_bytes=VMEM_LIMIT),
        name="ffn",
    )(x2, x2, x2, p2, g_ffn, w_up, conv_w, conv_b, w_down, w_ple, g_ple, w_ple_gate)


def _rope_tables(seq):
    t = jnp.arange(seq)
    row = (t // GRID_W).astype(F32)
    col = (t % GRID_W).astype(F32)
    n = HEAD_DIM // 2
    freqs = 1.0 / (ROPE_THETA ** (jnp.arange(0, n, 2, dtype=F32) / n))
    ang = jnp.concatenate([row[:, None] * freqs[None, :], col[:, None] * freqs[None, :]], axis=-1)
    cos = jnp.repeat(jnp.cos(ang), 2, axis=-1)
    sin = jnp.repeat(jnp.sin(ang), 2, axis=-1)
    sign = jnp.where(jnp.arange(HEAD_DIM) % 2 == 0, -1.0, 1.0).astype(F32)
    reps = LANES // HEAD_DIM
    return jnp.tile(cos, (1, reps)), jnp.tile(sin * sign[None, :], (1, reps))


def _head_mean_matrix():
    idx = jnp.arange(MXU_DIM) // HEAD_DIM
    return jnp.where(idx[:, None] == idx[None, :], 1.0 / HEAD_DIM, 0.0).astype(BF16)


def _dup_heads(w, n_heads):
    d = w.shape[0]
    w = w.reshape(d, n_heads, 1, HEAD_DIM)
    return jnp.broadcast_to(w, (d, n_heads, HEADS_PER_SLAB, HEAD_DIM)).reshape(d, n_heads * LANES)


def kernel(x, p, g_mix, w_in, qn_a, kn_a, rpb, qn_b, kn_b, w_gate, b_gate, w_br_a, w_br_b, w_out, g_ffn, w_up,
           conv_w, conv_b, w_down, w_ple, g_ple, w_ple_gate):
    batch, seq, d = x.shape
    depth = p.shape[0]
    rows = batch * seq
    na_w = w_br_a.shape[1]
    q_w = w_br_b.shape[1]
    n_kv = q_w // HEAD_DIM // GQA_GROUP
    kv_w = n_kv * HEAD_DIM
    assert seq == GRID_W * GRID_W and seq % ROW_TILE == 0 and rows % ROW_TILE == 0
    assert w_in.shape[2] == 3 * na_w + q_w + 2 * kv_w

    cos_t, sin_t = _rope_tables(seq)
    head_mean = _head_mean_matrix()
    na_bias = _natten_bias_call(rpb)
    q_scale = HEAD_DIM ** -0.5
    x2 = x.reshape(rows, d)

    for i in range(depth):
        o1, o2, o3 = na_w, 2 * na_w, 3 * na_w
        o4, o5 = o3 + q_w, o3 + q_w + kv_w
        wi = w_in[i]
        w_cat = jnp.concatenate(
            [wi[:, :o1], wi[:, o1:o2], wi[:, o3:o4], _dup_heads(wi[:, o4:o5], n_kv),
             wi[:, o2:o3], _dup_heads(wi[:, o5:], n_kv)], axis=1).astype(BF16)
        gains = jnp.concatenate(
            [jnp.tile(qn_a[i] * q_scale, na_w // HEAD_DIM), jnp.tile(kn_a[i], na_w // HEAD_DIM),
             jnp.tile(qn_b[i] * q_scale, q_w // HEAD_DIM), jnp.tile(kn_b[i], n_kv * HEADS_PER_SLAB)])[None, :]
        qa, ka, qb, kb, va, vb = _proj_call(x2, g_mix[i][None, :], w_cat, head_mean, gains.astype(F32),
                                            cos_t, sin_t, seq)
        out_a = _natten_call(qa, ka, va, na_bias, i, batch, seq)
        out_b = _gqa_call(qb, kb, vb, batch, seq)
        x2 = _merge_call(x2, g_mix[i][None, :], out_a, out_b, w_gate[i].astype(BF16), b_gate[i][None, :],
                         w_br_a[i].astype(BF16), w_br_b[i].astype(BF16), w_out[i].astype(BF16))
        x2 = _ffn_call(x2, p[i].reshape(rows, -1), g_ffn[i][None, :], w_up[i].astype(BF16), conv_w[i],
                       conv_b[i][None, :], w_down[i].astype(BF16), w_ple[i].astype(BF16), g_ple[i][None, :],
                       w_ple_gate[i].astype(BF16), seq)
    return x2.reshape(batch, seq, d)
```

```python
import functools

import jax
import jax.numpy as jnp
from jax import lax
from jax.experimental import pallas as pl
from jax.experimental.pallas import tpu as pltpu

F32 = jnp.float32
BF16 = jnp.bfloat16

HEAD_DIM = 64
GRID_W = 64
NA_KH = 8
NA_KW = 16
GQA_GROUP = 4
ROPE_THETA = 10000.0
EPS = 1e-6

LANES = 128
SUBLANES = 8
MXU_DIM = 256
HEADS_PER_SLAB = LANES // HEAD_DIM

NA_ROWS_PER_STEP = 4
NA_KEY_ROWS = 12
MASK_VALUE = -1e30

NA_GROUPS_PER_ITER = 4
LOG2_E = 1.4426950408889634

ROW_TILE = 512
PROJ_ROW_TILE = 1024
PROJ_SUB_ROWS = 512
GQA_Q_TILE = 512
GQA_K_TILE = 1024
FFN_CHUNKS = 2
VMEM_LIMIT = 56 * 1024 * 1024


def _rms_rows(x, g):
    ms = jnp.mean(x * x, axis=-1, keepdims=True)
    return x * lax.rsqrt(ms + EPS) * g


def _dot(a, b):
    return jnp.dot(a, b, preferred_element_type=F32)


def _dot_nt(a, b):
    return lax.dot_general(a, b, (((1,), (1,)), ((), ())), preferred_element_type=F32)


def _sigmoid(z):
    return 1.0 / (1.0 + jnp.exp(-z))


def _low_half_mask():
    return lax.broadcasted_iota(jnp.int32, (1, LANES), 1) < HEAD_DIM


def _proj_kernel(x_ref, g_ref, w_ref, wvt_ref, hm_ref, gn_ref, cos_ref, sin_ref,
                 qa_ref, ka_ref, qb_ref, kb_ref, va_ref, vt_ref, *, n_norm, n_rope_start):
    even_lane = (lax.broadcasted_iota(jnp.int32, (1, LANES), 1) % 2) == 0
    outs = ((qa_ref, 2), (ka_ref, 2), (qb_ref, 2), (kb_ref, 1), (va_ref, 2))
    slabs = [(o_ref, t) for o_ref, n_slabs in outs for t in range(n_slabs)]
    for r0 in range(0, x_ref.shape[0], PROJ_SUB_ROWS):
        r1 = r0 + PROJ_SUB_ROWS
        h = _rms_rows(x_ref[r0:r1, :], g_ref[...]).astype(BF16)
        ys = [_dot(h, w_ref[:, s * MXU_DIM:(s + 1) * MXU_DIM]) for s in range(len(slabs))]
        vt_ref[:, r0:r1] = _dot_nt(wvt_ref[...], h).astype(vt_ref.dtype)
        mss = [_dot((ys[s] * ys[s]).astype(BF16), hm_ref[...]) for s in range(n_norm)]
        for s, (o_ref, t) in enumerate(slabs):
            y = ys[s]
            if s < n_norm:
                y = y * lax.rsqrt(mss[s] + EPS) * gn_ref[:, s * MXU_DIM:(s + 1) * MXU_DIM]
                if s >= n_rope_start:
                    halves = []
                    for u in range(MXU_DIM // LANES):
                        yh = y[:, u * LANES:(u + 1) * LANES]
                        sw = jnp.where(even_lane, pltpu.roll(yh, LANES - 1, 1), pltpu.roll(yh, 1, 1))
                        halves.append(yh * cos_ref[r0:r1, :] + sw * sin_ref[r0:r1, :])
                    y = jnp.concatenate(halves, axis=1)
            o_ref[r0:r1, t * MXU_DIM:(t + 1) * MXU_DIM] = y.astype(o_ref.dtype)


def _proj_call(x2, g_mix, w_cat, w_vt, head_mean, gains, cos_t, sin_t, batch, seq):
    rows, d = x2.shape
    tm = PROJ_ROW_TILE
    tiles_per_seq = seq // tm
    kv_w = w_vt.shape[0]
    widths = (512, 512, 512, 256, 512)
    row_spec = lambda w: pl.BlockSpec((tm, w), lambda i: (i, 0))
    const = lambda a: pl.BlockSpec(a.shape, lambda i: (0, 0), pipeline_mode=pl.Buffered(1))
    rope_spec = pl.BlockSpec((tm, LANES), lambda i: (i % tiles_per_seq, 0))
    kern = functools.partial(_proj_kernel, n_norm=7, n_rope_start=4)
    return pl.pallas_call(
        kern,
        grid=(rows // tm,),
        in_specs=[row_spec(d), const(g_mix), const(w_cat), const(w_vt), const(head_mean), const(gains),
                  rope_spec, rope_spec],
        out_specs=[row_spec(w) for w in widths]
        + [pl.BlockSpec((None, None, kv_w, tm), lambda i: (i // tiles_per_seq, i % tiles_per_seq, 0, 0))],
        out_shape=[jax.ShapeDtypeStruct((rows, w), BF16) for w in widths]
        + [jax.ShapeDtypeStruct((batch, tiles_per_seq, kv_w, tm), BF16)],
        compiler_params=pltpu.CompilerParams(dimension_semantics=("parallel",), vmem_limit_bytes=VMEM_LIMIT),
        name="proj",
    )(x2, g_mix, w_cat, w_vt, head_mean, gains, cos_t, sin_t)


def _natten_kernel(q_ref, k_ref, v_ref, bias_ref, o_ref, *, n_groups, grid_rows):
    low = _low_half_mask()
    q_rows = NA_ROWS_PER_STEP * GRID_W
    k_rows = NA_KEY_ROWS * GRID_W

    def rows_of(g):
        r0 = g * NA_ROWS_PER_STEP
        k0 = jnp.clip(r0 - NA_KH // 2, 0, grid_rows - NA_KEY_ROWS)
        return pl.multiple_of(r0 * GRID_W, GRID_W), pl.multiple_of(k0 * GRID_W, GRID_W)

    def scores(g, par):
        qs, ks = rows_of(g)
        q2 = q_ref[pl.ds(qs, q_rows), :]
        sel = low if par == 0 else jnp.logical_not(low)
        qm = jnp.where(sel, q2, jnp.zeros_like(q2))
        var = jnp.where(g == 0, 0, jnp.where(g == n_groups - 1, 2, 1))
        return _dot_nt(qm, k_ref[pl.ds(ks, k_rows), :]) + bias_ref[par, var]

    def attend(g, s):
        _, ks = rows_of(g)
        m = jnp.max(s, axis=-1, keepdims=True)
        p = jnp.exp2(s - m)
        l = jnp.sum(p, axis=-1, keepdims=True)
        return _dot(p.astype(BF16), v_ref[pl.ds(ks, k_rows), :]) * (1.0 / l)

    def body(i, carry):
        tasks = [(i * NA_GROUPS_PER_ITER + u, par) for u in range(NA_GROUPS_PER_ITER)
                 for par in range(HEADS_PER_SLAB)]
        s = scores(*tasks[0])
        outs = []
        for t, (g, par) in enumerate(tasks):
            s_next = scores(*tasks[t + 1]) if t + 1 < len(tasks) else None
            outs.append(attend(g, s))
            s = s_next
            if par == HEADS_PER_SLAB - 1:
                qs, _ = rows_of(g)
                o_ref[pl.ds(qs, q_rows), :] = jnp.where(low, outs[0], outs[1]).astype(o_ref.dtype)
                outs = []
        return carry

    lax.fori_loop(0, n_groups // NA_GROUPS_PER_ITER, body, 0)


def _natten_call(qa, ka, va, bias, layer, batch, seq):
    rows, width = qa.shape
    n_slabs = width // LANES
    first = layer * n_slabs
    grid_rows = seq // GRID_W
    n_groups = grid_rows // NA_ROWS_PER_STEP
    slab_spec = pl.BlockSpec((seq, LANES), lambda j, b: (b, j))
    kern = functools.partial(_natten_kernel, n_groups=n_groups, grid_rows=grid_rows)
    return pl.pallas_call(
        kern,
        grid=(n_slabs, batch),
        in_specs=[slab_spec, slab_spec, slab_spec,
                  pl.BlockSpec((HEADS_PER_SLAB,) + bias.shape[1:], lambda j, b: (first + j, 0, 0, 0))],
        out_specs=slab_spec,
        out_shape=jax.ShapeDtypeStruct((rows, width), BF16),
        compiler_params=pltpu.CompilerParams(dimension_semantics=("parallel", "parallel"),
                                             vmem_limit_bytes=VMEM_LIMIT),
        name="natten",
    )(qa, ka, va, bias)


def _natten_bias_kernel(w_ref, o_ref, *, grid_rows):
    n_pairs = w_ref.shape[1]
    c = lax.broadcasted_iota(jnp.int32, (GRID_W, LANES), 0)
    lane = lax.broadcasted_iota(jnp.int32, (GRID_W, LANES), 1)
    kc = lane % GRID_W
    cs = jnp.clip(c - NA_KW // 2, 0, GRID_W - NA_KW)
    col_valid = (kc >= cs) & (kc < cs + NA_KW)
    upper = (lane >= GRID_W).astype(jnp.int32)
    masked = jnp.full((GRID_W, LANES), MASK_VALUE, F32)
    lo, hi = NA_KH // 2 - 1, NA_KH // 2 + NA_KH - 2
    pairs, pairs_interior = [], []
    for a in range(n_pairs):
        x = jnp.broadcast_to(w_ref[0, a:a + 1, :], (GRID_W, LANES))
        y = pltpu.roll(x, LANES - (NA_KW - 1), 1, stride=1, stride_axis=0)
        blk = jnp.where(col_valid, y * LOG2_E, masked)
        row_idx = a + upper
        pairs.append(blk)
        pairs_interior.append(jnp.where((row_idx >= lo) & (row_idx <= hi), blk, masked))
    r0s = (0, NA_ROWS_PER_STEP, grid_rows - NA_ROWS_PER_STEP)
    for var, r0 in enumerate(r0s):
        k0 = min(max(r0 - NA_KH // 2, 0), grid_rows - NA_KEY_ROWS)
        rs = [min(max(r0 + j - NA_KH // 2, 0), grid_rows - NA_KH) for j in range(NA_ROWS_PER_STEP)]
        for j in range(NA_ROWS_PER_STEP):
            for t in range(NA_KEY_ROWS // 2):
                a = 2 * t - j + (k0 - r0) + NA_KH - 1
                rows_ok = [rs[j] <= k0 + rel < rs[j] + NA_KH for rel in (2 * t, 2 * t + 1)]
                if var == 1:
                    blk = pairs_interior[a]
                elif all(rows_ok):
                    blk = pairs[a]
                else:
                    assert not any(rows_ok)
                    blk = masked
                o_ref[0, var, j * GRID_W:(j + 1) * GRID_W, t * LANES:(t + 1) * LANES] = blk


def _natten_bias_call(rpb):
    n_l, n_h, n_r, n_c = rpb.shape
    assert n_r == 2 * NA_KH - 1 and n_c == 2 * NA_KW - 1 and n_c <= GRID_W
    rp = jnp.pad(rpb.astype(F32), ((0, 0), (0, 0), (0, 0), (0, GRID_W - n_c)))
    pairs = jnp.concatenate([rp[:, :, :-1], rp[:, :, 1:]], axis=-1).reshape(n_l * n_h, n_r - 1, LANES)
    q_rows, k_rows = NA_ROWS_PER_STEP * GRID_W, NA_KEY_ROWS * GRID_W
    kern = functools.partial(_natten_bias_kernel, grid_rows=GRID_W)
    return pl.pallas_call(
        kern,
        grid=(n_l * n_h,),
        in_specs=[pl.BlockSpec((1, n_r - 1, LANES), lambda h: (h, 0, 0))],
        out_specs=pl.BlockSpec((1, 3, q_rows, k_rows), lambda h: (h, 0, 0, 0)),
        out_shape=jax.ShapeDtypeStruct((n_l * n_h, 3, q_rows, k_rows), F32),
        compiler_params=pltpu.CompilerParams(dimension_semantics=("parallel",), vmem_limit_bytes=VMEM_LIMIT),
        name="natten_bias",
    )(pairs)


def _gqa_kernel(q_ref, k_ref, vt_ref, o_ref, m_ref, l_ref, acc_ref):
    low = _low_half_mask()
    tk = GQA_K_TILE
    per_block = vt_ref.shape[2] // tk
    n_chunks = vt_ref.shape[0] * per_block
    n_heads = m_ref.shape[0]
    m_ref[...] = jnp.full(m_ref.shape, MASK_VALUE, F32)
    l_ref[...] = jnp.zeros(l_ref.shape, F32)
    acc_ref[...] = jnp.zeros(acc_ref.shape, F32)

    def scores(c, hd):
        j, par = divmod(hd, HEADS_PER_SLAB)
        q2 = q_ref[:, j * LANES:(j + 1) * LANES]
        sel = low if par == 0 else jnp.logical_not(low)
        qm = jnp.where(sel, q2, jnp.zeros_like(q2))
        return _dot_nt(k_ref[c * tk:(c + 1) * tk, :], qm)

    def update(c, hd, st):
        m_prev = m_ref[hd]
        m_new = jnp.maximum(m_prev, jnp.max(st, axis=0, keepdims=True))
        alpha = jnp.exp2(m_prev - m_new)
        pt = jnp.exp2(st - m_new)
        l_ref[hd] = alpha * l_ref[hd] + jnp.sum(pt, axis=0, keepdims=True)
        b, u = divmod(c, per_block)
        vc = vt_ref[b, :, u * tk:(u + 1) * tk]
        acc_ref[hd] = alpha * acc_ref[hd] + _dot(vc, pt.astype(BF16))
        m_ref[hd] = m_new

    tasks = [(c, hd) for c in range(n_chunks) for hd in range(n_heads)]
    st = scores(*tasks[0])
    for t, task in enumerate(tasks):
        st_next = scores(*tasks[t + 1]) if t + 1 < len(tasks) else None
        update(*task, st)
        st = st_next
    for j in range(n_heads // HEADS_PER_SLAB):
        outs_t = [acc_ref[hd] * (1.0 / l_ref[hd]) for hd in range(j * HEADS_PER_SLAB, (j + 1) * HEADS_PER_SLAB)]
        slab = jnp.concatenate(outs_t, axis=0).T
        o_ref[:, j * LANES:(j + 1) * LANES] = slab.astype(o_ref.dtype)


def _gqa_call(qb, kb, vbt, batch, seq):
    rows, width = qb.shape
    n_kv = kb.shape[1] // LANES
    n_chunks, tk = vbt.shape[1], vbt.shape[3]
    tq = GQA_Q_TILE
    q_width = width // n_kv
    n_heads = q_width // HEAD_DIM
    blocks = seq // tq
    q_spec = pl.BlockSpec((tq, q_width), lambda b, h, i: (b * blocks + i, h))
    k_spec = pl.BlockSpec((seq, LANES), lambda b, h, i: (b, h))
    vt_spec = pl.BlockSpec((None, n_chunks, HEAD_DIM, tk), lambda b, h, i: (b, 0, h, 0))
    return pl.pallas_call(
        _gqa_kernel,
        grid=(batch, n_kv, blocks),
        in_specs=[q_spec, k_spec, vt_spec],
        out_specs=q_spec,
        out_shape=jax.ShapeDtypeStruct((rows, width), BF16),
        scratch_shapes=[pltpu.VMEM((n_heads, 1, tq), F32), pltpu.VMEM((n_heads, 1, tq), F32),
                        pltpu.VMEM((n_heads, HEAD_DIM, tq), F32)],
        compiler_params=pltpu.CompilerParams(dimension_semantics=("parallel", "parallel", "parallel"),
                                             vmem_limit_bytes=VMEM_LIMIT),
        name="gqa",
    )(qb, kb, vbt)


def _merge_kernel(x_ref, g_ref, oa_ref, ob_ref, wg_ref, bg_ref, wa_ref, wb_ref, wo_ref, o_ref):
    x = x_ref[...]
    d = x.shape[1]
    h = _rms_rows(x, g_ref[...]).astype(BF16)
    gates = _sigmoid(_dot(h, wg_ref[...]) + bg_ref[...])
    merged = gates[:, :d] * _dot(oa_ref[...], wa_ref[...]) + gates[:, d:] * _dot(ob_ref[...], wb_ref[...])
    o_ref[...] = x + _dot(merged.astype(BF16), wo_ref[...])


def _merge_call(x2, g_mix, out_a, out_b, w_gate, b_gate, w_br_a, w_br_b, w_out):
    rows, d = x2.shape
    tm = ROW_TILE
    row_spec = lambda w: pl.BlockSpec((tm, w), lambda i: (i, 0))
    const = lambda a: pl.BlockSpec(a.shape, lambda i: (0, 0))
    return pl.pallas_call(
        _merge_kernel,
        grid=(rows // tm,),
        in_specs=[row_spec(d), const(g_mix), row_spec(out_a.shape[1]), row_spec(out_b.shape[1]),
                  const(w_gate), const(b_gate), const(w_br_a), const(w_br_b), const(w_out)],
        out_specs=row_spec(d),
        out_shape=jax.ShapeDtypeStruct((rows, d), F32),
        compiler_params=pltpu.CompilerParams(dimension_semantics=("parallel",), vmem_limit_bytes=VMEM_LIMIT),
        name="merge",
    )(x2, g_mix, out_a, out_b, w_gate, b_gate, w_br_a, w_br_b, w_out)


def _ffn_kernel(x_ref, xp_ref, xn_ref, p_ref, gf_ref, wu_ref, cw_ref, cb_ref, wd_ref, wp_ref, gp_ref, wpg_ref,
                o_ref, ug_ref, uv_ref, *, tiles_per_seq, n_chunks):
    i = pl.program_id(0)
    tm, d = x_ref.shape
    d_ff = wd_ref.shape[0]
    fc = d_ff // n_chunks
    x = x_ref[...]
    gf = gf_ref[...]
    h = _rms_rows(x, gf).astype(BF16)
    prev_ok = ((i % tiles_per_seq) != 0).astype(F32)
    next_ok = (((i + 1) % tiles_per_seq) != 0).astype(F32)
    h_halo = _rms_rows(jnp.concatenate([xp_ref[...], xn_ref[...]], axis=0), gf).astype(BF16)

    def conv_part(u_ref, c0):
        w = wu_ref[:, c0:c0 + fc]
        u = _dot(h, w)
        uh = _dot(h_halo, w)
        u_ref[0:SUBLANES, :] = uh[0:SUBLANES] * prev_ok
        u_ref[SUBLANES:SUBLANES + tm, :] = u
        u_ref[SUBLANES + tm:2 * SUBLANES + tm, :] = uh[SUBLANES:] * next_ok
        cw = cw_ref[:, c0:c0 + fc]
        return (cw[0:1] * u_ref[SUBLANES - 1:SUBLANES - 1 + tm, :] + cw[1:2] * u
                + cw[2:3] * u_ref[SUBLANES + 1:SUBLANES + 1 + tm, :] + cb_ref[:, c0:c0 + fc])

    acc = jnp.zeros((tm, d), F32)
    for c in range(n_chunks):
        yg = conv_part(ug_ref, c * fc)
        yv = conv_part(uv_ref, d_ff + c * fc)
        act = (yg * _sigmoid(yg) * yv).astype(BF16)
        acc = acc + _dot(act, wd_ref[c * fc:(c + 1) * fc, :])
    x2 = x + acc
    e = _dot(p_ref[...].astype(BF16), wp_ref[...])
    gate = _sigmoid(_dot(_rms_rows(x2, gp_ref[...]).astype(BF16), wpg_ref[...]))
    o_ref[...] = x2 + gate * e


def _ffn_call(x2, p2, g_ffn, w_up, conv_w, conv_b, w_down, w_ple, g_ple, w_ple_gate, seq):
    rows, d = x2.shape
    tm = ROW_TILE
    d_ff = w_down.shape[0]
    n_chunks = FFN_CHUNKS
    assert d_ff % (n_chunks * LANES) == 0
    fc = d_ff // n_chunks
    tiles_per_seq = seq // tm
    blocks8 = tm // SUBLANES
    last8 = rows // SUBLANES - 1
    row_spec = lambda w: pl.BlockSpec((tm, w), lambda i: (i, 0))
    const = lambda a: pl.BlockSpec(a.shape, lambda i: (0, 0), pipeline_mode=pl.Buffered(1))
    kern = functools.partial(_ffn_kernel, tiles_per_seq=tiles_per_seq, n_chunks=n_chunks)
    return pl.pallas_call(
        kern,
        grid=(rows // tm,),
        in_specs=[
            row_spec(d),
            pl.BlockSpec((SUBLANES, d), lambda i: (jnp.maximum(i * blocks8 - 1, 0), 0)),
            pl.BlockSpec((SUBLANES, d), lambda i: (jnp.minimum((i + 1) * blocks8, last8), 0)),
            row_spec(p2.shape[1]),
            const(g_ffn), const(w_up), const(conv_w), const(conv_b), const(w_down),
            const(w_ple), const(g_ple), const(w_ple_gate),
        ],
        out_specs=row_spec(d),
        out_shape=jax.ShapeDtypeStruct((rows, d), F32),
        scratch_shapes=[pltpu.VMEM((tm + 2 * SUBLANES, fc), F32), pltpu.VMEM((tm + 2 * SUBLANES, fc), F32)],
        compiler_params=pltpu.CompilerParams(dimension_semantics=("parallel",), vmem_limit_bytes=VMEM_LIMIT),
        name="ffn",
    )(x2, x2, x2, p2, g_ffn, w_up, conv_w, conv_b, w_down, w_ple, g_ple, w_ple_gate)


def _rope_tables(seq):
    t = jnp.arange(seq)
    row = (t // GRID_W).astype(F32)
    col = (t % GRID_W).astype(F32)
    n = HEAD_DIM // 2
    freqs = 1.0 / (ROPE_THETA ** (jnp.arange(0, n, 2, dtype=F32) / n))
    ang = jnp.concatenate([row[:, None] * freqs[None, :], col[:, None] * freqs[None, :]], axis=-1)
    cos = jnp.repeat(jnp.cos(ang), 2, axis=-1)
    sin = jnp.repeat(jnp.sin(ang), 2, axis=-1)
    sign = jnp.where(jnp.arange(HEAD_DIM) % 2 == 0, -1.0, 1.0).astype(F32)
    reps = LANES // HEAD_DIM
    return jnp.tile(cos, (1, reps)), jnp.tile(sin * sign[None, :], (1, reps))


def _head_mean_matrix():
    idx = jnp.arange(MXU_DIM) // HEAD_DIM
    return jnp.where(idx[:, None] == idx[None, :], 1.0 / HEAD_DIM, 0.0).astype(BF16)


def _dup_heads(w, n_heads):
    d = w.shape[0]
    w = w.reshape(d, n_heads, 1, HEAD_DIM)
    return jnp.broadcast_to(w, (d, n_heads, HEADS_PER_SLAB, HEAD_DIM)).reshape(d, n_heads * LANES)


def kernel(x, p, g_mix, w_in, qn_a, kn_a, rpb, qn_b, kn_b, w_gate, b_gate, w_br_a, w_br_b, w_out, g_ffn, w_up,
           conv_w, conv_b, w_down, w_ple, g_ple, w_ple_gate):
    batch, seq, d = x.shape
    depth = p.shape[0]
    rows = batch * seq
    na_w = w_br_a.shape[1]
    q_w = w_br_b.shape[1]
    n_kv = q_w // HEAD_DIM // GQA_GROUP
    kv_w = n_kv * HEAD_DIM
    assert seq == GRID_W * GRID_W and seq % ROW_TILE == 0 and seq % PROJ_ROW_TILE == 0 and seq % GQA_Q_TILE == 0
    assert w_in.shape[2] == 3 * na_w + q_w + 2 * kv_w

    cos_t, sin_t = _rope_tables(seq)
    head_mean = _head_mean_matrix()
    na_bias = _natten_bias_call(rpb)
    q_scale = HEAD_DIM ** -0.5 * LOG2_E
    x2 = x.reshape(rows, d)

    for i in range(depth):
        o1, o2, o3 = na_w, 2 * na_w, 3 * na_w
        o4, o5 = o3 + q_w, o3 + q_w + kv_w
        wi = w_in[i]
        w_cat = jnp.concatenate(
            [wi[:, :o1], wi[:, o1:o2], wi[:, o3:o4], _dup_heads(wi[:, o4:o5], n_kv), wi[:, o2:o3]],
            axis=1).astype(BF16)
        w_vt = wi[:, o5:].T.astype(BF16)
        gains = jnp.concatenate(
            [jnp.tile(qn_a[i] * q_scale, na_w // HEAD_DIM), jnp.tile(kn_a[i], na_w // HEAD_DIM),
             jnp.tile(qn_b[i] * q_scale, q_w // HEAD_DIM), jnp.tile(kn_b[i], n_kv * HEADS_PER_SLAB)])[None, :]
        qa, ka, qb, kb, va, vbt = _proj_call(x2, g_mix[i][None, :], w_cat, w_vt, head_mean, gains.astype(F32),
                                             cos_t, sin_t, batch, seq)
        out_a = _natten_call(qa, ka, va, na_bias, i, batch, seq)
        out_b = _gqa_call(qb, kb, vbt, batch, seq)
        x2 = _merge_call(x2, g_mix[i][None, :], out_a, out_b, w_gate[i].astype(BF16), b_gate[i][None, :],
                         w_br_a[i].astype(BF16), w_br_b[i].astype(BF16), w_out[i].astype(BF16))
        x2 = _ffn_call(x2, p[i].reshape(rows, -1), g_ffn[i][None, :], w_up[i].astype(BF16), conv_w[i],
                       conv_b[i][None, :], w_down[i].astype(BF16), w_ple[i].astype(BF16), g_ple[i][None, :],
                       w_ple_gate[i].astype(BF16), seq)
    return x2.reshape(batch, seq, d)
```

```python
import functools

import jax
import jax.numpy as jnp
from jax import lax
from jax.experimental import pallas as pl
from jax.experimental.pallas import tpu as pltpu

F32 = jnp.float32
BF16 = jnp.bfloat16

HEAD_DIM = 64
GRID_W = 64
NA_KH = 8
NA_KW = 16
GQA_GROUP = 4
ROPE_THETA = 10000.0
EPS = 1e-6

LANES = 128
SUBLANES = 8
MXU_DIM = 256
HEADS_PER_SLAB = LANES // HEAD_DIM

NA_ROWS_PER_STEP = 4
NA_KEY_ROWS = 12
MASK_VALUE = -1e30

NA_GROUPS_PER_ITER = 4
LOG2_E = 1.4426950408889634

ROW_TILE = 512
PROJ_ROW_TILE = 1024
PROJ_SUB_ROWS = 512
GQA_Q_TILE = 512
GQA_K_TILE = 1024
FFN_CHUNKS = 2
GQA_BOUND_SLACK = 1.0 + 2.0 ** -6
GQA_MIN_SUM = 2.0 ** -60
VMEM_LIMIT = 56 * 1024 * 1024


def _rms_rows(x, g):
    ms = jnp.mean(x * x, axis=-1, keepdims=True)
    return x * lax.rsqrt(ms + EPS) * g


def _dot(a, b):
    return jnp.dot(a, b, preferred_element_type=F32)


def _dot_nt(a, b):
    return lax.dot_general(a, b, (((1,), (1,)), ((), ())), preferred_element_type=F32)


def _sigmoid(z):
    return 1.0 / (1.0 + jnp.exp2(z * -LOG2_E))


def _low_half_mask():
    return lax.broadcasted_iota(jnp.int32, (1, LANES), 1) < HEAD_DIM


def _proj_kernel(x_ref, g_ref, w_ref, wvt_ref, hm_ref, gn_ref, cos_ref, sin_ref,
                 qa_ref, ka_ref, qb_ref, kb_ref, va_ref, vt_ref, *, n_norm, n_rope_start):
    even_lane = (lax.broadcasted_iota(jnp.int32, (1, LANES), 1) % 2) == 0
    outs = ((qa_ref, 2), (ka_ref, 2), (qb_ref, 2), (kb_ref, 1), (va_ref, 2))
    slabs = [(o_ref, t) for o_ref, n_slabs in outs for t in range(n_slabs)]
    for r0 in range(0, x_ref.shape[0], PROJ_SUB_ROWS):
        r1 = r0 + PROJ_SUB_ROWS
        h = _rms_rows(x_ref[r0:r1, :], g_ref[...]).astype(BF16)
        ys = [_dot(h, w_ref[:, s * MXU_DIM:(s + 1) * MXU_DIM]) for s in range(len(slabs))]
        vt_ref[:, r0:r1] = _dot_nt(wvt_ref[...], h).astype(vt_ref.dtype)
        mss = [_dot((ys[s] * ys[s]).astype(BF16), hm_ref[...]) for s in range(n_norm)]
        for s, (o_ref, t) in enumerate(slabs):
            y = ys[s]
            if s < n_norm:
                y = y * lax.rsqrt(mss[s] + EPS) * gn_ref[:, s * MXU_DIM:(s + 1) * MXU_DIM]
                if s >= n_rope_start:
                    halves = []
                    for u in range(MXU_DIM // LANES):
                        yh = y[:, u * LANES:(u + 1) * LANES]
                        sw = jnp.where(even_lane, pltpu.roll(yh, LANES - 1, 1), pltpu.roll(yh, 1, 1))
                        halves.append(yh * cos_ref[r0:r1, :] + sw * sin_ref[r0:r1, :])
                    y = jnp.concatenate(halves, axis=1)
            o_ref[r0:r1, t * MXU_DIM:(t + 1) * MXU_DIM] = y.astype(o_ref.dtype)


def _proj_call(x2, g_mix, w_cat, w_vt, head_mean, gains, cos_t, sin_t, batch, seq):
    rows, d = x2.shape
    tm = PROJ_ROW_TILE
    tiles_per_seq = seq // tm
    kv_w = w_vt.shape[0]
    widths = (512, 512, 512, 256, 512)
    row_spec = lambda w: pl.BlockSpec((tm, w), lambda i: (i, 0))
    const = lambda a: pl.BlockSpec(a.shape, lambda i: (0, 0), pipeline_mode=pl.Buffered(1))
    rope_spec = pl.BlockSpec((tm, LANES), lambda i: (i % tiles_per_seq, 0))
    kern = functools.partial(_proj_kernel, n_norm=7, n_rope_start=4)
    return pl.pallas_call(
        kern,
        grid=(rows // tm,),
        in_specs=[row_spec(d), const(g_mix), const(w_cat), const(w_vt), const(head_mean), const(gains),
                  rope_spec, rope_spec],
        out_specs=[row_spec(w) for w in widths]
        + [pl.BlockSpec((None, None, kv_w, tm), lambda i: (i // tiles_per_seq, i % tiles_per_seq, 0, 0))],
        out_shape=[jax.ShapeDtypeStruct((rows, w), BF16) for w in widths]
        + [jax.ShapeDtypeStruct((batch, tiles_per_seq, kv_w, tm), BF16)],
        compiler_params=pltpu.CompilerParams(dimension_semantics=("parallel",), vmem_limit_bytes=VMEM_LIMIT),
        name="proj",
    )(x2, g_mix, w_cat, w_vt, head_mean, gains, cos_t, sin_t)


def _natten_kernel(q_ref, k_ref, v_ref, bias_ref, o_ref, *, n_groups, grid_rows):
    low = _low_half_mask()
    q_rows = NA_ROWS_PER_STEP * GRID_W
    k_rows = NA_KEY_ROWS * GRID_W

    def rows_of(g):
        r0 = g * NA_ROWS_PER_STEP
        k0 = jnp.clip(r0 - NA_KH // 2, 0, grid_rows - NA_KEY_ROWS)
        return pl.multiple_of(r0 * GRID_W, GRID_W), pl.multiple_of(k0 * GRID_W, GRID_W)

    def scores(g, par):
        qs, ks = rows_of(g)
        q2 = q_ref[pl.ds(qs, q_rows), :]
        sel = low if par == 0 else jnp.logical_not(low)
        qm = jnp.where(sel, q2, jnp.zeros_like(q2))
        var = jnp.where(g == 0, 0, jnp.where(g == n_groups - 1, 2, 1))
        return _dot_nt(qm, k_ref[pl.ds(ks, k_rows), :]) + bias_ref[par, var]

    def attend(g, s):
        _, ks = rows_of(g)
        m = jnp.max(s, axis=-1, keepdims=True)
        p = jnp.exp2(s - m)
        l = jnp.sum(p, axis=-1, keepdims=True)
        return _dot(p.astype(BF16), v_ref[pl.ds(ks, k_rows), :]) * (1.0 / l)

    def body(i, carry):
        tasks = [(i * NA_GROUPS_PER_ITER + u, par) for u in range(NA_GROUPS_PER_ITER)
                 for par in range(HEADS_PER_SLAB)]
        s = scores(*tasks[0])
        outs = []
        for t, (g, par) in enumerate(tasks):
            s_next = scores(*tasks[t + 1]) if t + 1 < len(tasks) else None
            outs.append(attend(g, s))
            s = s_next
            if par == HEADS_PER_SLAB - 1:
                qs, _ = rows_of(g)
                o_ref[pl.ds(qs, q_rows), :] = jnp.where(low, outs[0], outs[1]).astype(o_ref.dtype)
                outs = []
        return carry

    lax.fori_loop(0, n_groups // NA_GROUPS_PER_ITER, body, 0)


def _natten_call(qa, ka, va, bias, layer, batch, seq):
    rows, width = qa.shape
    n_slabs = width // LANES
    first = layer * n_slabs
    grid_rows = seq // GRID_W
    n_groups = grid_rows // NA_ROWS_PER_STEP
    slab_spec = pl.BlockSpec((seq, LANES), lambda j, b: (b, j))
    kern = functools.partial(_natten_kernel, n_groups=n_groups, grid_rows=grid_rows)
    return pl.pallas_call(
        kern,
        grid=(n_slabs, batch),
        in_specs=[slab_spec, slab_spec, slab_spec,
                  pl.BlockSpec((HEADS_PER_SLAB,) + bias.shape[1:], lambda j, b: (first + j, 0, 0, 0))],
        out_specs=slab_spec,
        out_shape=jax.ShapeDtypeStruct((rows, width), BF16),
        compiler_params=pltpu.CompilerParams(dimension_semantics=("parallel", "parallel"),
                                             vmem_limit_bytes=VMEM_LIMIT),
        name="natten",
    )(qa, ka, va, bias)


def _natten_bias_kernel(w_ref, o_ref, *, grid_rows):
    n_pairs = w_ref.shape[1]
    c = lax.broadcasted_iota(jnp.int32, (GRID_W, LANES), 0)
    lane = lax.broadcasted_iota(jnp.int32, (GRID_W, LANES), 1)
    kc = lane % GRID_W
    cs = jnp.clip(c - NA_KW // 2, 0, GRID_W - NA_KW)
    col_valid = (kc >= cs) & (kc < cs + NA_KW)
    upper = (lane >= GRID_W).astype(jnp.int32)
    masked = jnp.full((GRID_W, LANES), MASK_VALUE, F32)
    lo, hi = NA_KH // 2 - 1, NA_KH // 2 + NA_KH - 2
    pairs, pairs_interior = [], []
    for a in range(n_pairs):
        x = jnp.broadcast_to(w_ref[0, a:a + 1, :], (GRID_W, LANES))
        y = pltpu.roll(x, LANES - (NA_KW - 1), 1, stride=1, stride_axis=0)
        blk = jnp.where(col_valid, y * LOG2_E, masked)
        row_idx = a + upper
        pairs.append(blk)
        pairs_interior.append(jnp.where((row_idx >= lo) & (row_idx <= hi), blk, masked))
    r0s = (0, NA_ROWS_PER_STEP, grid_rows - NA_ROWS_PER_STEP)
    for var, r0 in enumerate(r0s):
        k0 = min(max(r0 - NA_KH // 2, 0), grid_rows - NA_KEY_ROWS)
        rs = [min(max(r0 + j - NA_KH // 2, 0), grid_rows - NA_KH) for j in range(NA_ROWS_PER_STEP)]
        for j in range(NA_ROWS_PER_STEP):
            for t in range(NA_KEY_ROWS // 2):
                a = 2 * t - j + (k0 - r0) + NA_KH - 1
                rows_ok = [rs[j] <= k0 + rel < rs[j] + NA_KH for rel in (2 * t, 2 * t + 1)]
                if var == 1:
                    blk = pairs_interior[a]
                elif all(rows_ok):
                    blk = pairs[a]
                else:
                    assert not any(rows_ok)
                    blk = masked
                o_ref[0, var, j * GRID_W:(j + 1) * GRID_W, t * LANES:(t + 1) * LANES] = blk


def _natten_bias_call(rpb):
    n_l, n_h, n_r, n_c = rpb.shape
    assert n_r == 2 * NA_KH - 1 and n_c == 2 * NA_KW - 1 and n_c <= GRID_W
    rp = jnp.pad(rpb.astype(F32), ((0, 0), (0, 0), (0, 0), (0, GRID_W - n_c)))
    pairs = jnp.concatenate([rp[:, :, :-1], rp[:, :, 1:]], axis=-1).reshape(n_l * n_h, n_r - 1, LANES)
    q_rows, k_rows = NA_ROWS_PER_STEP * GRID_W, NA_KEY_ROWS * GRID_W
    kern = functools.partial(_natten_bias_kernel, grid_rows=GRID_W)
    return pl.pallas_call(
        kern,
        grid=(n_l * n_h,),
        in_specs=[pl.BlockSpec((1, n_r - 1, LANES), lambda h: (h, 0, 0))],
        out_specs=pl.BlockSpec((1, 3, q_rows, k_rows), lambda h: (h, 0, 0, 0)),
        out_shape=jax.ShapeDtypeStruct((n_l * n_h, 3, q_rows, k_rows), F32),
        compiler_params=pltpu.CompilerParams(dimension_semantics=("parallel",), vmem_limit_bytes=VMEM_LIMIT),
        name="natten_bias",
    )(pairs)


def _gqa_kernel(q_ref, k_ref, vt_ref, o_ref, m_ref, l_ref, acc_ref, kmax_ref):
    low = _low_half_mask()
    tk = GQA_K_TILE
    per_block = vt_ref.shape[2] // tk
    n_chunks = vt_ref.shape[0] * per_block
    n_heads = m_ref.shape[0]
    ones = jnp.ones((SUBLANES, LANES), BF16)

    def row_sq_norms(a):
        af = a.astype(F32)
        return _dot_nt(ones, (af * af).astype(BF16))[0:1]

    @pl.when(pl.program_id(2) == 0)
    def _():
        kmax = jnp.max(row_sq_norms(k_ref[...]), axis=1, keepdims=True) * 0.5
        kmax_ref[...] = jnp.broadcast_to(kmax, kmax_ref.shape)

    def masked_q(hd):
        j, par = divmod(hd, HEADS_PER_SLAB)
        q2 = q_ref[:, j * LANES:(j + 1) * LANES]
        sel = low if par == 0 else jnp.logical_not(low)
        return jnp.where(sel, q2, jnp.zeros_like(q2))

    def scores(c, hd):
        return _dot_nt(k_ref[c * tk:(c + 1) * tk, :], masked_q(hd))

    def values(c):
        b, u = divmod(c, per_block)
        return vt_ref[b, :, u * tk:(u + 1) * tk]

    def update_bounded(c, hd, st):
        pt = jnp.exp2(st - m_ref[hd])
        l_ref[hd] = l_ref[hd] + jnp.sum(pt, axis=0, keepdims=True)
        acc_ref[hd] = acc_ref[hd] + _dot(values(c), pt.astype(BF16))

    def update_online(c, hd, st):
        m_prev = m_ref[hd]
        m_new = jnp.maximum(m_prev, jnp.max(st, axis=0, keepdims=True))
        alpha = jnp.exp2(m_prev - m_new)
        pt = jnp.exp2(st - m_new)
        l_ref[hd] = alpha * l_ref[hd] + jnp.sum(pt, axis=0, keepdims=True)
        acc_ref[hd] = alpha * acc_ref[hd] + _dot(values(c), pt.astype(BF16))
        m_ref[hd] = m_new

    def run(update):
        l_ref[...] = jnp.zeros(l_ref.shape, F32)
        acc_ref[...] = jnp.zeros(acc_ref.shape, F32)
        tasks = [(c, hd) for c in range(n_chunks) for hd in range(n_heads)]
        st = scores(*tasks[0])
        for t, task in enumerate(tasks):
            st_next = scores(*tasks[t + 1]) if t + 1 < len(tasks) else None
            update(*task, st)
            st = st_next

    def store():
        for j in range(n_heads // HEADS_PER_SLAB):
            outs_t = [acc_ref[hd] * (1.0 / l_ref[hd]) for hd in range(j * HEADS_PER_SLAB, (j + 1) * HEADS_PER_SLAB)]
            slab = jnp.concatenate(outs_t, axis=0).T
            o_ref[:, j * LANES:(j + 1) * LANES] = slab.astype(o_ref.dtype)

    for hd in range(n_heads):
        m_ref[hd] = jnp.sqrt(row_sq_norms(masked_q(hd)) * kmax_ref[...]) * GQA_BOUND_SLACK
    run(update_bounded)
    trusted = jnp.min(l_ref[...]) >= GQA_MIN_SUM

    @pl.when(trusted)
    def _():
        store()

    @pl.when(jnp.logical_not(trusted))
    def _():
        m_ref[...] = jnp.full(m_ref.shape, MASK_VALUE, F32)
        run(update_online)
        store()


def _gqa_call(qb, kb, vbt, batch, seq):
    rows, width = qb.shape
    n_kv = kb.shape[1] // LANES
    n_chunks, tk = vbt.shape[1], vbt.shape[3]
    tq = GQA_Q_TILE
    q_width = width // n_kv
    n_heads = q_width // HEAD_DIM
    blocks = seq // tq
    q_spec = pl.BlockSpec((tq, q_width), lambda b, h, i: (b * blocks + i, h))
    k_spec = pl.BlockSpec((seq, LANES), lambda b, h, i: (b, h))
    vt_spec = pl.BlockSpec((None, n_chunks, HEAD_DIM, tk), lambda b, h, i: (b, 0, h, 0))
    return pl.pallas_call(
        _gqa_kernel,
        grid=(batch, n_kv, blocks),
        in_specs=[q_spec, k_spec, vt_spec],
        out_specs=q_spec,
        out_shape=jax.ShapeDtypeStruct((rows, width), BF16),
        scratch_shapes=[pltpu.VMEM((n_heads, 1, tq), F32), pltpu.VMEM((n_heads, 1, tq), F32),
                        pltpu.VMEM((n_heads, HEAD_DIM, tq), F32), pltpu.VMEM((1, tq), F32)],
        compiler_params=pltpu.CompilerParams(dimension_semantics=("parallel", "parallel", "arbitrary"),
                                             vmem_limit_bytes=VMEM_LIMIT),
        name="gqa",
    )(qb, kb, vbt)


def _merge_kernel(x_ref, g_ref, oa_ref, ob_ref, wg_ref, bg_ref, wa_ref, wb_ref, wo_ref, o_ref):
    x = x_ref[...]
    d = x.shape[1]
    h = _rms_rows(x, g_ref[...]).astype(BF16)
    gates = _sigmoid(_dot(h, wg_ref[...]) + bg_ref[...])
    merged = gates[:, :d] * _dot(oa_ref[...], wa_ref[...]) + gates[:, d:] * _dot(ob_ref[...], wb_ref[...])
    o_ref[...] = x + _dot(merged.astype(BF16), wo_ref[...])


def _merge_call(x2, g_mix, out_a, out_b, w_gate, b_gate, w_br_a, w_br_b, w_out):
    rows, d = x2.shape
    tm = ROW_TILE
    row_spec = lambda w: pl.BlockSpec((tm, w), lambda i: (i, 0))
    const = lambda a: pl.BlockSpec(a.shape, lambda i: (0, 0))
    return pl.pallas_call(
        _merge_kernel,
        grid=(rows // tm,),
        in_specs=[row_spec(d), const(g_mix), row_spec(out_a.shape[1]), row_spec(out_b.shape[1]),
                  const(w_gate), const(b_gate), const(w_br_a), const(w_br_b), const(w_out)],
        out_specs=row_spec(d),
        out_shape=jax.ShapeDtypeStruct((rows, d), F32),
        compiler_params=pltpu.CompilerParams(dimension_semantics=("parallel",), vmem_limit_bytes=VMEM_LIMIT),
        name="merge",
    )(x2, g_mix, out_a, out_b, w_gate, b_gate, w_br_a, w_br_b, w_out)


def _ffn_kernel(x_ref, xp_ref, xn_ref, p_ref, gf_ref, wu_ref, cw_ref, cb_ref, wd_ref, wp_ref, gp_ref, wpg_ref,
                o_ref, ug_ref, uv_ref, *, tiles_per_seq, n_chunks):
    i = pl.program_id(0)
    tm, d = x_ref.shape
    d_ff = wd_ref.shape[0]
    fc = d_ff // n_chunks
    x = x_ref[...]
    gf = gf_ref[...]
    h = _rms_rows(x, gf).astype(BF16)
    prev_ok = ((i % tiles_per_seq) != 0).astype(F32)
    next_ok = (((i + 1) % tiles_per_seq) != 0).astype(F32)
    h_halo = _rms_rows(jnp.concatenate([xp_ref[...], xn_ref[...]], axis=0), gf).astype(BF16)

    def conv_part(u_ref, c0):
        w = wu_ref[:, c0:c0 + fc]
        u = _dot(h, w)
        uh = _dot(h_halo, w)
        u_ref[0:SUBLANES, :] = uh[0:SUBLANES] * prev_ok
        u_ref[SUBLANES:SUBLANES + tm, :] = u
        u_ref[SUBLANES + tm:2 * SUBLANES + tm, :] = uh[SUBLANES:] * next_ok
        cw = cw_ref[:, c0:c0 + fc]
        return (cw[0:1] * u_ref[SUBLANES - 1:SUBLANES - 1 + tm, :] + cw[1:2] * u
                + cw[2:3] * u_ref[SUBLANES + 1:SUBLANES + 1 + tm, :] + cb_ref[:, c0:c0 + fc])

    acc = jnp.zeros((tm, d), F32)
    for c in range(n_chunks):
        yg = conv_part(ug_ref, c * fc)
        yv = conv_part(uv_ref, d_ff + c * fc)
        act = (yg * _sigmoid(yg) * yv).astype(BF16)
        acc = acc + _dot(act, wd_ref[c * fc:(c + 1) * fc, :])
    x2 = x + acc
    e = _dot(p_ref[...].astype(BF16), wp_ref[...])
    gate = _sigmoid(_dot(_rms_rows(x2, gp_ref[...]).astype(BF16), wpg_ref[...]))
    o_ref[...] = x2 + gate * e


def _ffn_call(x2, p2, g_ffn, w_up, conv_w, conv_b, w_down, w_ple, g_ple, w_ple_gate, seq):
    rows, d = x2.shape
    tm = ROW_TILE
    d_ff = w_down.shape[0]
    n_chunks = FFN_CHUNKS
    assert d_ff % (n_chunks * LANES) == 0
    fc = d_ff // n_chunks
    tiles_per_seq = seq // tm
    blocks8 = tm // SUBLANES
    last8 = rows // SUBLANES - 1
    row_spec = lambda w: pl.BlockSpec((tm, w), lambda i: (i, 0))
    const = lambda a: pl.BlockSpec(a.shape, lambda i: (0, 0), pipeline_mode=pl.Buffered(1))
    kern = functools.partial(_ffn_kernel, tiles_per_seq=tiles_per_seq, n_chunks=n_chunks)
    return pl.pallas_call(
        kern,
        grid=(rows // tm,),
        in_specs=[
            row_spec(d),
            pl.BlockSpec((SUBLANES, d), lambda i: (jnp.maximum(i * blocks8 - 1, 0), 0)),
            pl.BlockSpec((SUBLANES, d), lambda i: (jnp.minimum((i + 1) * blocks8, last8), 0)),
            row_spec(p2.shape[1]),
            const(g_ffn), const(w_up), const(conv_w), const(conv_b), const(w_down),
            const(w_ple), const(g_ple), const(w_ple_gate),
        ],
        out_specs=row_spec(d),
        out_shape=jax.ShapeDtypeStruct((rows, d), F32),
        scratch_shapes=[pltpu.VMEM((tm + 2 * SUBLANES, fc), F32), pltpu.VMEM((tm + 2 * SUBLANES, fc), F32)],
        compiler_params=pltpu.CompilerParams(dimension_semantics=("parallel",), vmem_limit_bytes=VMEM_LIMIT),
        name="ffn",
    )(x2, x2, x2, p2, g_ffn, w_up, conv_w, conv_b, w_down, w_ple, g_ple, w_ple_gate)


def _rope_tables(seq):
    t = jnp.arange(seq)
    row = (t // GRID_W).astype(F32)
    col = (t % GRID_W).astype(F32)
    n = HEAD_DIM // 2
    freqs = 1.0 / (ROPE_THETA ** (jnp.arange(0, n, 2, dtype=F32) / n))
    ang = jnp.concatenate([row[:, None] * freqs[None, :], col[:, None] * freqs[None, :]], axis=-1)
    cos = jnp.repeat(jnp.cos(ang), 2, axis=-1)
    sin = jnp.repeat(jnp.sin(ang), 2, axis=-1)
    sign = jnp.where(jnp.arange(HEAD_DIM) % 2 == 0, -1.0, 1.0).astype(F32)
    reps = LANES // HEAD_DIM
    return jnp.tile(cos, (1, reps)), jnp.tile(sin * sign[None, :], (1, reps))


def _head_mean_matrix():
    idx = jnp.arange(MXU_DIM) // HEAD_DIM
    return jnp.where(idx[:, None] == idx[None, :], 1.0 / HEAD_DIM, 0.0).astype(BF16)


def _dup_heads(w, n_heads):
    d = w.shape[0]
    w = w.reshape(d, n_heads, 1, HEAD_DIM)
    return jnp.broadcast_to(w, (d, n_heads, HEADS_PER_SLAB, HEAD_DIM)).reshape(d, n_heads * LANES)


def kernel(x, p, g_mix, w_in, qn_a, kn_a, rpb, qn_b, kn_b, w_gate, b_gate, w_br_a, w_br_b, w_out, g_ffn, w_up,
           conv_w, conv_b, w_down, w_ple, g_ple, w_ple_gate):
    batch, seq, d = x.shape
    depth = p.shape[0]
    rows = batch * seq
    na_w = w_br_a.shape[1]
    q_w = w_br_b.shape[1]
    n_kv = q_w // HEAD_DIM // GQA_GROUP
    kv_w = n_kv * HEAD_DIM
    assert seq == GRID_W * GRID_W and seq % ROW_TILE == 0 and seq % PROJ_ROW_TILE == 0 and seq % GQA_Q_TILE == 0
    assert w_in.shape[2] == 3 * na_w + q_w + 2 * kv_w

    cos_t, sin_t = _rope_tables(seq)
    head_mean = _head_mean_matrix()
    na_bias = _natten_bias_call(rpb)
    q_scale = HEAD_DIM ** -0.5 * LOG2_E
    x2 = x.reshape(rows, d)

    for i in range(depth):
        o1, o2, o3 = na_w, 2 * na_w, 3 * na_w
        o4, o5 = o3 + q_w, o3 + q_w + kv_w
        wi = w_in[i]
        w_cat = jnp.concatenate(
            [wi[:, :o1], wi[:, o1:o2], wi[:, o3:o4], _dup_heads(wi[:, o4:o5], n_kv), wi[:, o2:o3]],
            axis=1).astype(BF16)
        w_vt = wi[:, o5:].T.astype(BF16)
        gains = jnp.concatenate(
            [jnp.tile(qn_a[i] * q_scale, na_w // HEAD_DIM), jnp.tile(kn_a[i], na_w // HEAD_DIM),
             jnp.tile(qn_b[i] * q_scale, q_w // HEAD_DIM), jnp.tile(kn_b[i], n_kv * HEADS_PER_SLAB)])[None, :]
        qa, ka, qb, kb, va, vbt = _proj_call(x2, g_mix[i][None, :], w_cat, w_vt, head_mean, gains.astype(F32),
                                             cos_t, sin_t, batch, seq)
        out_a = _natten_call(qa, ka, va, na_bias, i, batch, seq)
        out_b = _gqa_call(qb, kb, vbt, batch, seq)
        x2 = _merge_call(x2, g_mix[i][None, :], out_a, out_b, w_gate[i].astype(BF16), b_gate[i][None, :],
                         w_br_a[i].astype(BF16), w_br_b[i].astype(BF16), w_out[i].astype(BF16))
        x2 = _ffn_call(x2, p[i].reshape(rows, -1), g_ffn[i][None, :], w_up[i].astype(BF16), conv_w[i],
                       conv_b[i][None, :], w_down[i].astype(BF16), w_ple[i].astype(BF16), g_ple[i][None, :],
                       w_ple_gate[i].astype(BF16), seq)
    return x2.reshape(batch, seq, d)
```

```python
import functools

import jax
import jax.numpy as jnp
from jax import lax
from jax.experimental import pallas as pl
from jax.experimental.pallas import tpu as pltpu

F32 = jnp.float32
BF16 = jnp.bfloat16

HEAD_DIM = 64
GRID_W = 64
NA_KH = 8
NA_KW = 16
GQA_GROUP = 4
ROPE_THETA = 10000.0
EPS = 1e-6

LANES = 128
SUBLANES = 8
MXU_DIM = 256
HEADS_PER_SLAB = LANES // HEAD_DIM

NA_ROWS_PER_STEP = 4
NA_KEY_ROWS = 12
MASK_VALUE = -1e30

NA_GROUPS_PER_ITER = 8
LOG2_E = 1.4426950408889634

ROW_TILE = 512
PROJ_ROW_TILE = 1024
PROJ_SUB_ROWS = 512
GQA_Q_TILE = 512
GQA_K_TILE = 1024
GQA_LOOKAHEAD = 1
FFN_CHUNKS = 1
GQA_BOUND_SLACK = 1.0 + 2.0 ** -6
GQA_MIN_SUM = 2.0 ** -60
VMEM_LIMIT = 56 * 1024 * 1024


def _rms_rows(x, g):
    ms = jnp.mean(x * x, axis=-1, keepdims=True)
    return x * lax.rsqrt(ms + EPS) * g


def _dot(a, b):
    return jnp.dot(a, b, preferred_element_type=F32)


def _dot_nt(a, b):
    return lax.dot_general(a, b, (((1,), (1,)), ((), ())), preferred_element_type=F32)


def _sigmoid(z):
    return 1.0 / (1.0 + jnp.exp2(z * -LOG2_E))


def _layer_spec(a, layer):
    return pl.BlockSpec((None,) + a.shape[1:], lambda *_: (layer, 0, 0), pipeline_mode=pl.Buffered(1))


def _low_half_mask():
    return lax.broadcasted_iota(jnp.int32, (1, LANES), 1) < HEAD_DIM


def _proj_kernel(x_ref, g_ref, w_ref, wvt_ref, hm_ref, gn_ref, cos_ref, sin_ref,
                 qa_ref, ka_ref, qb_ref, kb_ref, va_ref, vt_ref, *, n_norm, n_rope_start):
    even_lane = (lax.broadcasted_iota(jnp.int32, (1, LANES), 1) % 2) == 0
    outs = ((qa_ref, 2), (ka_ref, 2), (qb_ref, 2), (kb_ref, 1), (va_ref, 2))
    slabs = [(o_ref, t) for o_ref, n_slabs in outs for t in range(n_slabs)]
    for r0 in range(0, x_ref.shape[0], PROJ_SUB_ROWS):
        r1 = r0 + PROJ_SUB_ROWS
        h = _rms_rows(x_ref[r0:r1, :], g_ref[...]).astype(BF16)
        ys = [_dot(h, w_ref[:, s * MXU_DIM:(s + 1) * MXU_DIM]) for s in range(len(slabs))]
        vt_ref[:, r0:r1] = _dot_nt(wvt_ref[...], h).astype(vt_ref.dtype)
        mss = [_dot((ys[s] * ys[s]).astype(BF16), hm_ref[...]) for s in range(n_norm)]
        for s, (o_ref, t) in enumerate(slabs):
            y = ys[s]
            if s < n_norm:
                y = y * lax.rsqrt(mss[s] + EPS) * gn_ref[:, s * MXU_DIM:(s + 1) * MXU_DIM]
                if s >= n_rope_start:
                    halves = []
                    for u in range(MXU_DIM // LANES):
                        yh = y[:, u * LANES:(u + 1) * LANES]
                        sw = jnp.where(even_lane, pltpu.roll(yh, LANES - 1, 1), pltpu.roll(yh, 1, 1))
                        halves.append(yh * cos_ref[r0:r1, :] + sw * sin_ref[r0:r1, :])
                    y = jnp.concatenate(halves, axis=1)
            o_ref[r0:r1, t * MXU_DIM:(t + 1) * MXU_DIM] = y.astype(o_ref.dtype)


def _proj_call(x2, g_mix, w_cat, w_vt, head_mean, gains, cos_t, sin_t, layer, batch, seq):
    rows, d = x2.shape
    tm = PROJ_ROW_TILE
    tiles_per_seq = seq // tm
    kv_w = w_vt.shape[1]
    widths = (512, 512, 512, 256, 512)
    row_spec = lambda w: pl.BlockSpec((tm, w), lambda i: (i, 0))
    per_layer = lambda a: _layer_spec(a, layer)
    rope_spec = pl.BlockSpec((tm, LANES), lambda i: (i % tiles_per_seq, 0))
    kern = functools.partial(_proj_kernel, n_norm=7, n_rope_start=4)
    return pl.pallas_call(
        kern,
        grid=(rows // tm,),
        in_specs=[row_spec(d), per_layer(g_mix), per_layer(w_cat), per_layer(w_vt),
                  pl.BlockSpec(head_mean.shape, lambda i: (0, 0), pipeline_mode=pl.Buffered(1)), per_layer(gains),
                  rope_spec, rope_spec],
        out_specs=[row_spec(w) for w in widths]
        + [pl.BlockSpec((None, None, kv_w, tm), lambda i: (i // tiles_per_seq, i % tiles_per_seq, 0, 0))],
        out_shape=[jax.ShapeDtypeStruct((rows, w), BF16) for w in widths]
        + [jax.ShapeDtypeStruct((batch, tiles_per_seq, kv_w, tm), BF16)],
        compiler_params=pltpu.CompilerParams(dimension_semantics=("parallel",), vmem_limit_bytes=VMEM_LIMIT),
        name="proj",
    )(x2, g_mix, w_cat, w_vt, head_mean, gains, cos_t, sin_t)


def _natten_kernel(q_ref, k_ref, v_ref, bias_ref, o_ref, *, n_groups, grid_rows):
    low = _low_half_mask()
    q_rows = NA_ROWS_PER_STEP * GRID_W
    k_rows = NA_KEY_ROWS * GRID_W

    def rows_of(g):
        r0 = g * NA_ROWS_PER_STEP
        k0 = jnp.clip(r0 - NA_KH // 2, 0, grid_rows - NA_KEY_ROWS)
        return pl.multiple_of(r0 * GRID_W, GRID_W), pl.multiple_of(k0 * GRID_W, GRID_W)

    def scores(g, par):
        qs, ks = rows_of(g)
        q2 = q_ref[pl.ds(qs, q_rows), :]
        sel = low if par == 0 else jnp.logical_not(low)
        qm = jnp.where(sel, q2, jnp.zeros_like(q2))
        var = jnp.where(g == 0, 0, jnp.where(g == n_groups - 1, 2, 1))
        return _dot_nt(qm, k_ref[pl.ds(ks, k_rows), :]) + bias_ref[par, var]

    def attend(g, s):
        _, ks = rows_of(g)
        m = jnp.max(s, axis=-1, keepdims=True)
        p = jnp.exp2(s - m)
        l = jnp.sum(p, axis=-1, keepdims=True)
        return _dot(p.astype(BF16), v_ref[pl.ds(ks, k_rows), :]) * (1.0 / l)

    def body(i, carry):
        tasks = [(i * NA_GROUPS_PER_ITER + u, par) for u in range(NA_GROUPS_PER_ITER)
                 for par in range(HEADS_PER_SLAB)]
        s = scores(*tasks[0])
        outs = []
        for t, (g, par) in enumerate(tasks):
            s_next = scores(*tasks[t + 1]) if t + 1 < len(tasks) else None
            outs.append(attend(g, s))
            s = s_next
            if par == HEADS_PER_SLAB - 1:
                qs, _ = rows_of(g)
                o_ref[pl.ds(qs, q_rows), :] = jnp.where(low, outs[0], outs[1]).astype(o_ref.dtype)
                outs = []
        return carry

    lax.fori_loop(0, n_groups // NA_GROUPS_PER_ITER, body, 0)


def _natten_call(qa, ka, va, bias, layer, batch, seq):
    rows, width = qa.shape
    n_slabs = width // LANES
    first = layer * n_slabs
    grid_rows = seq // GRID_W
    n_groups = grid_rows // NA_ROWS_PER_STEP
    slab_spec = pl.BlockSpec((seq, LANES), lambda j, b: (b, j))
    kern = functools.partial(_natten_kernel, n_groups=n_groups, grid_rows=grid_rows)
    return pl.pallas_call(
        kern,
        grid=(n_slabs, batch),
        in_specs=[slab_spec, slab_spec, slab_spec,
                  pl.BlockSpec((HEADS_PER_SLAB,) + bias.shape[1:], lambda j, b: (first + j, 0, 0, 0))],
        out_specs=slab_spec,
        out_shape=jax.ShapeDtypeStruct((rows, width), BF16),
        compiler_params=pltpu.CompilerParams(dimension_semantics=("parallel", "parallel"),
                                             vmem_limit_bytes=VMEM_LIMIT),
        name="natten",
    )(qa, ka, va, bias)


def _natten_bias_kernel(w_ref, o_ref, *, grid_rows):
    n_pairs = w_ref.shape[1]
    c = lax.broadcasted_iota(jnp.int32, (GRID_W, LANES), 0)
    lane = lax.broadcasted_iota(jnp.int32, (GRID_W, LANES), 1)
    kc = lane % GRID_W
    cs = jnp.clip(c - NA_KW // 2, 0, GRID_W - NA_KW)
    col_valid = (kc >= cs) & (kc < cs + NA_KW)
    upper = (lane >= GRID_W).astype(jnp.int32)
    masked = jnp.full((GRID_W, LANES), MASK_VALUE, F32)
    lo, hi = NA_KH // 2 - 1, NA_KH // 2 + NA_KH - 2
    pairs, pairs_interior = [], []
    for a in range(n_pairs):
        x = jnp.broadcast_to(w_ref[0, a:a + 1, :], (GRID_W, LANES))
        y = pltpu.roll(x, LANES - (NA_KW - 1), 1, stride=1, stride_axis=0)
        blk = jnp.where(col_valid, y * LOG2_E, masked)
        row_idx = a + upper
        pairs.append(blk)
        pairs_interior.append(jnp.where((row_idx >= lo) & (row_idx <= hi), blk, masked))
    r0s = (0, NA_ROWS_PER_STEP, grid_rows - NA_ROWS_PER_STEP)
    for var, r0 in enumerate(r0s):
        k0 = min(max(r0 - NA_KH // 2, 0), grid_rows - NA_KEY_ROWS)
        rs = [min(max(r0 + j - NA_KH // 2, 0), grid_rows - NA_KH) for j in range(NA_ROWS_PER_STEP)]
        for j in range(NA_ROWS_PER_STEP):
            for t in range(NA_KEY_ROWS // 2):
                a = 2 * t - j + (k0 - r0) + NA_KH - 1
                rows_ok = [rs[j] <= k0 + rel < rs[j] + NA_KH for rel in (2 * t, 2 * t + 1)]
                if var == 1:
                    blk = pairs_interior[a]
                elif all(rows_ok):
                    blk = pairs[a]
                else:
                    assert not any(rows_ok)
                    blk = masked
                o_ref[0, var, j * GRID_W:(j + 1) * GRID_W, t * LANES:(t + 1) * LANES] = blk


def _natten_bias_call(rpb):
    n_l, n_h, n_r, n_c = rpb.shape
    assert n_r == 2 * NA_KH - 1 and n_c == 2 * NA_KW - 1 and n_c <= GRID_W
    rp = jnp.pad(rpb.astype(F32), ((0, 0), (0, 0), (0, 0), (0, GRID_W - n_c)))
    pairs = jnp.concatenate([rp[:, :, :-1], rp[:, :, 1:]], axis=-1).reshape(n_l * n_h, n_r - 1, LANES)
    q_rows, k_rows = NA_ROWS_PER_STEP * GRID_W, NA_KEY_ROWS * GRID_W
    kern = functools.partial(_natten_bias_kernel, grid_rows=GRID_W)
    return pl.pallas_call(
        kern,
        grid=(n_l * n_h,),
        in_specs=[pl.BlockSpec((1, n_r - 1, LANES), lambda h: (h, 0, 0))],
        out_specs=pl.BlockSpec((1, 3, q_rows, k_rows), lambda h: (h, 0, 0, 0)),
        out_shape=jax.ShapeDtypeStruct((n_l * n_h, 3, q_rows, k_rows), F32),
        compiler_params=pltpu.CompilerParams(dimension_semantics=("parallel",), vmem_limit_bytes=VMEM_LIMIT),
        name="natten_bias",
    )(pairs)


def _gqa_kernel(q_ref, k_ref, vt_ref, o_ref, m_ref, l_ref, acc_ref, kmax_ref):
    low = _low_half_mask()
    tk = GQA_K_TILE
    per_block = vt_ref.shape[2] // tk
    n_chunks = vt_ref.shape[0] * per_block
    n_heads = m_ref.shape[0]
    ones = jnp.ones((SUBLANES, LANES), BF16)

    def row_sq_norms(a):
        af = a.astype(F32)
        return _dot_nt(ones, (af * af).astype(BF16))[0:1]

    @pl.when(pl.program_id(2) == 0)
    def _():
        kmax = jnp.max(row_sq_norms(k_ref[...]), axis=1, keepdims=True) * 0.5
        kmax_ref[...] = jnp.broadcast_to(kmax, kmax_ref.shape)

    def masked_q(hd):
        j, par = divmod(hd, HEADS_PER_SLAB)
        q2 = q_ref[:, j * LANES:(j + 1) * LANES]
        sel = low if par == 0 else jnp.logical_not(low)
        return jnp.where(sel, q2, jnp.zeros_like(q2))

    def scores(c, hd):
        return _dot_nt(k_ref[c * tk:(c + 1) * tk, :], masked_q(hd))

    def values(c):
        b, u = divmod(c, per_block)
        return vt_ref[b, :, u * tk:(u + 1) * tk]

    def update_bounded(c, hd, st):
        pt = jnp.exp2(st - m_ref[hd])
        l_ref[hd] = l_ref[hd] + jnp.sum(pt, axis=0, keepdims=True)
        acc_ref[hd] = acc_ref[hd] + _dot(values(c), pt.astype(BF16))

    def update_online(c, hd, st):
        m_prev = m_ref[hd]
        m_new = jnp.maximum(m_prev, jnp.max(st, axis=0, keepdims=True))
        alpha = jnp.exp2(m_prev - m_new)
        pt = jnp.exp2(st - m_new)
        l_ref[hd] = alpha * l_ref[hd] + jnp.sum(pt, axis=0, keepdims=True)
        acc_ref[hd] = alpha * acc_ref[hd] + _dot(values(c), pt.astype(BF16))
        m_ref[hd] = m_new

    def run(update):
        l_ref[...] = jnp.zeros(l_ref.shape, F32)
        acc_ref[...] = jnp.zeros(acc_ref.shape, F32)
        tasks = [(c, hd) for c in range(n_chunks) for hd in range(n_heads)]
        pending = [scores(*task) for task in tasks[:GQA_LOOKAHEAD]]
        for t, task in enumerate(tasks):
            if t + GQA_LOOKAHEAD < len(tasks):
                pending.append(scores(*tasks[t + GQA_LOOKAHEAD]))
            update(*task, pending.pop(0))

    def store():
        for j in range(n_heads // HEADS_PER_SLAB):
            outs_t = [acc_ref[hd] * (1.0 / l_ref[hd]) for hd in range(j * HEADS_PER_SLAB, (j + 1) * HEADS_PER_SLAB)]
            slab = jnp.concatenate(outs_t, axis=0).T
            o_ref[:, j * LANES:(j + 1) * LANES] = slab.astype(o_ref.dtype)

    for hd in range(n_heads):
        m_ref[hd] = jnp.sqrt(row_sq_norms(masked_q(hd)) * kmax_ref[...]) * GQA_BOUND_SLACK
    run(update_bounded)
    trusted = jnp.min(l_ref[...]) >= GQA_MIN_SUM

    @pl.when(trusted)
    def _():
        store()

    @pl.when(jnp.logical_not(trusted))
    def _():
        m_ref[...] = jnp.full(m_ref.shape, MASK_VALUE, F32)
        run(update_online)
        store()


def _gqa_call(qb, kb, vbt, batch, seq):
    rows, width = qb.shape
    n_kv = kb.shape[1] // LANES
    n_chunks, tk = vbt.shape[1], vbt.shape[3]
    tq = GQA_Q_TILE
    q_width = width // n_kv
    n_heads = q_width // HEAD_DIM
    blocks = seq // tq
    q_spec = pl.BlockSpec((tq, q_width), lambda b, h, i: (b * blocks + i, h))
    k_spec = pl.BlockSpec((seq, LANES), lambda b, h, i: (b, h))
    vt_spec = pl.BlockSpec((None, n_chunks, HEAD_DIM, tk), lambda b, h, i: (b, 0, h, 0))
    return pl.pallas_call(
        _gqa_kernel,
        grid=(batch, n_kv, blocks),
        in_specs=[q_spec, k_spec, vt_spec],
        out_specs=q_spec,
        out_shape=jax.ShapeDtypeStruct((rows, width), BF16),
        scratch_shapes=[pltpu.VMEM((n_heads, 1, tq), F32), pltpu.VMEM((n_heads, 1, tq), F32),
                        pltpu.VMEM((n_heads, HEAD_DIM, tq), F32), pltpu.VMEM((1, tq), F32)],
        compiler_params=pltpu.CompilerParams(dimension_semantics=("parallel", "parallel", "arbitrary"),
                                             vmem_limit_bytes=VMEM_LIMIT),
        name="gqa",
    )(qb, kb, vbt)


def _merge_kernel(x_ref, g_ref, oa_ref, ob_ref, wg_ref, bg_ref, wa_ref, wb_ref, wo_ref, o_ref):
    x = x_ref[...]
    d = x.shape[1]
    h = _rms_rows(x, g_ref[...]).astype(BF16)
    gates = _sigmoid(_dot(h, wg_ref[...]) + bg_ref[...])
    merged = gates[:, :d] * _dot(oa_ref[...], wa_ref[...]) + gates[:, d:] * _dot(ob_ref[...], wb_ref[...])
    o_ref[...] = x + _dot(merged.astype(BF16), wo_ref[...])


def _merge_call(x2, g_mix, out_a, out_b, w_gate, b_gate, w_br_a, w_br_b, w_out, layer):
    rows, d = x2.shape
    tm = ROW_TILE
    row_spec = lambda w: pl.BlockSpec((tm, w), lambda i: (i, 0))
    const = lambda a: _layer_spec(a, layer)
    return pl.pallas_call(
        _merge_kernel,
        grid=(rows // tm,),
        in_specs=[row_spec(d), const(g_mix), row_spec(out_a.shape[1]), row_spec(out_b.shape[1]),
                  const(w_gate), const(b_gate), const(w_br_a), const(w_br_b), const(w_out)],
        out_specs=row_spec(d),
        out_shape=jax.ShapeDtypeStruct((rows, d), F32),
        compiler_params=pltpu.CompilerParams(dimension_semantics=("parallel",), vmem_limit_bytes=VMEM_LIMIT),
        name="merge",
    )(x2, g_mix, out_a, out_b, w_gate, b_gate, w_br_a, w_br_b, w_out)


def _ffn_kernel(x_ref, xp_ref, xn_ref, p_ref, gf_ref, wu_ref, cw_ref, cb_ref, wd_ref, wp_ref, gp_ref, wpg_ref,
                o_ref, ug_ref, uv_ref, *, tiles_per_seq, n_chunks):
    i = pl.program_id(0)
    tm, d = x_ref.shape
    d_ff = wd_ref.shape[0]
    fc = d_ff // n_chunks
    x = x_ref[...]
    gf = gf_ref[...]
    h = _rms_rows(x, gf).astype(BF16)
    prev_ok = ((i % tiles_per_seq) != 0).astype(F32)
    next_ok = (((i + 1) % tiles_per_seq) != 0).astype(F32)
    h_halo = _rms_rows(jnp.concatenate([xp_ref[...], xn_ref[...]], axis=0), gf).astype(BF16)

    def conv_part(u_ref, c0):
        w = wu_ref[:, c0:c0 + fc]
        u = _dot(h, w)
        uh = _dot(h_halo, w)
        u_ref[0:SUBLANES, :] = uh[0:SUBLANES] * prev_ok
        u_ref[SUBLANES:SUBLANES + tm, :] = u
        u_ref[SUBLANES + tm:2 * SUBLANES + tm, :] = uh[SUBLANES:] * next_ok
        cw = cw_ref[:, c0:c0 + fc]
        return (cw[0:1] * u_ref[SUBLANES - 1:SUBLANES - 1 + tm, :] + cw[1:2] * u
                + cw[2:3] * u_ref[SUBLANES + 1:SUBLANES + 1 + tm, :] + cb_ref[:, c0:c0 + fc])

    acc = jnp.zeros((tm, d), F32)
    for c in range(n_chunks):
        yg = conv_part(ug_ref, c * fc)
        yv = conv_part(uv_ref, d_ff + c * fc)
        act = (yg * _sigmoid(yg) * yv).astype(BF16)
        acc = acc + _dot(act, wd_ref[c * fc:(c + 1) * fc, :])
    x2 = x + acc
    e = _dot(p_ref[...].astype(BF16), wp_ref[...])
    gate = _sigmoid(_dot(_rms_rows(x2, gp_ref[...]).astype(BF16), wpg_ref[...]))
    o_ref[...] = x2 + gate * e


def _ffn_call(x2, p2, g_ffn, w_up, conv_w, conv_b, w_down, w_ple, g_ple, w_ple_gate, layer, seq):
    rows, d = x2.shape
    tm = ROW_TILE
    d_ff = w_down.shape[1]
    first_tile = layer * (rows // tm)
    n_chunks = FFN_CHUNKS
    assert d_ff % (n_chunks * LANES) == 0
    fc = d_ff // n_chunks
    tiles_per_seq = seq // tm
    blocks8 = tm // SUBLANES
    last8 = rows // SUBLANES - 1
    row_spec = lambda w: pl.BlockSpec((tm, w), lambda i: (i, 0))
    const = lambda a: _layer_spec(a, layer)
    kern = functools.partial(_ffn_kernel, tiles_per_seq=tiles_per_seq, n_chunks=n_chunks)
    return pl.pallas_call(
        kern,
        grid=(rows // tm,),
        in_specs=[
            row_spec(d),
            pl.BlockSpec((SUBLANES, d), lambda i: (jnp.maximum(i * blocks8 - 1, 0), 0)),
            pl.BlockSpec((SUBLANES, d), lambda i: (jnp.minimum((i + 1) * blocks8, last8), 0)),
            pl.BlockSpec((tm, p2.shape[1]), lambda i: (first_tile + i, 0)),
            const(g_ffn), const(w_up), const(conv_w), const(conv_b), const(w_down),
            const(w_ple), const(g_ple), const(w_ple_gate),
        ],
        out_specs=row_spec(d),
        out_shape=jax.ShapeDtypeStruct((rows, d), F32),
        scratch_shapes=[pltpu.VMEM((tm + 2 * SUBLANES, fc), F32), pltpu.VMEM((tm + 2 * SUBLANES, fc), F32)],
        compiler_params=pltpu.CompilerParams(dimension_semantics=("parallel",), vmem_limit_bytes=VMEM_LIMIT),
        name="ffn",
    )(x2, x2, x2, p2, g_ffn, w_up, conv_w, conv_b, w_down, w_ple, g_ple, w_ple_gate)


def _rope_tables(seq):
    t = jnp.arange(seq)
    row = (t // GRID_W).astype(F32)
    col = (t % GRID_W).astype(F32)
    n = HEAD_DIM // 2
    freqs = 1.0 / (ROPE_THETA ** (jnp.arange(0, n, 2, dtype=F32) / n))
    ang = jnp.concatenate([row[:, None] * freqs[None, :], col[:, None] * freqs[None, :]], axis=-1)
    cos = jnp.repeat(jnp.cos(ang), 2, axis=-1)
    sin = jnp.repeat(jnp.sin(ang), 2, axis=-1)
    sign = jnp.where(jnp.arange(HEAD_DIM) % 2 == 0, -1.0, 1.0).astype(F32)
    reps = LANES // HEAD_DIM
    return jnp.tile(cos, (1, reps)), jnp.tile(sin * sign[None, :], (1, reps))


def _head_mean_matrix():
    idx = jnp.arange(MXU_DIM) // HEAD_DIM
    return jnp.where(idx[:, None] == idx[None, :], 1.0 / HEAD_DIM, 0.0).astype(BF16)


def _dup_heads(w, n_heads):
    lead = w.shape[:-1]
    w = w.reshape(lead + (n_heads, 1, HEAD_DIM))
    return jnp.broadcast_to(w, lead + (n_heads, HEADS_PER_SLAB, HEAD_DIM)).reshape(lead + (n_heads * LANES,))


def kernel(x, p, g_mix, w_in, qn_a, kn_a, rpb, qn_b, kn_b, w_gate, b_gate, w_br_a, w_br_b, w_out, g_ffn, w_up,
           conv_w, conv_b, w_down, w_ple, g_ple, w_ple_gate):
    batch, seq, d = x.shape
    depth = p.shape[0]
    rows = batch * seq
    na_w = w_br_a.shape[1]
    q_w = w_br_b.shape[1]
    n_kv = q_w // HEAD_DIM // GQA_GROUP
    kv_w = n_kv * HEAD_DIM
    assert seq == GRID_W * GRID_W and seq % ROW_TILE == 0 and seq % PROJ_ROW_TILE == 0 and seq % GQA_Q_TILE == 0
    assert w_in.shape[2] == 3 * na_w + q_w + 2 * kv_w

    cos_t, sin_t = _rope_tables(seq)
    head_mean = _head_mean_matrix()
    na_bias = _natten_bias_call(rpb)
    q_scale = HEAD_DIM ** -0.5 * LOG2_E
    x2 = x.reshape(rows, d)

    o1, o2, o3 = na_w, 2 * na_w, 3 * na_w
    o4, o5 = o3 + q_w, o3 + q_w + kv_w
    w_cat = jnp.concatenate(
        [w_in[:, :, :o1], w_in[:, :, o1:o2], w_in[:, :, o3:o4], _dup_heads(w_in[:, :, o4:o5], n_kv),
         w_in[:, :, o2:o3]], axis=2).astype(BF16)
    w_vt = jnp.swapaxes(w_in[:, :, o5:], 1, 2).astype(BF16)
    tile = lambda g, n: jnp.tile(g, (1, n))
    gains = jnp.concatenate(
        [tile(qn_a * q_scale, na_w // HEAD_DIM), tile(kn_a, na_w // HEAD_DIM),
         tile(qn_b * q_scale, q_w // HEAD_DIM), tile(kn_b, n_kv * HEADS_PER_SLAB)], axis=1).astype(F32)
    row3 = lambda a: a[:, None, :]
    bf = lambda a: a.astype(BF16)
    g_mix3, gains3, b_gate3 = row3(g_mix), row3(gains), row3(b_gate)
    w_gate_b, w_br_a_b, w_br_b_b, w_out_b = bf(w_gate), bf(w_br_a), bf(w_br_b), bf(w_out)
    g_ffn3, conv_b3, g_ple3 = row3(g_ffn), row3(conv_b), row3(g_ple)
    w_up_b, w_down_b, w_ple_b, w_ple_gate_b = bf(w_up), bf(w_down), bf(w_ple), bf(w_ple_gate)
    p2 = p.reshape(depth * rows, p.shape[-1])

    for i in range(depth):
        qa, ka, qb, kb, va, vbt = _proj_call(x2, g_mix3, w_cat, w_vt, head_mean, gains3, cos_t, sin_t, i, batch, seq)
        out_a = _natten_call(qa, ka, va, na_bias, i, batch, seq)
        out_b = _gqa_call(qb, kb, vbt, batch, seq)
        x2 = _merge_call(x2, g_mix3, out_a, out_b, w_gate_b, b_gate3, w_br_a_b, w_br_b_b, w_out_b, i)
        x2 = _ffn_call(x2, p2, g_ffn3, w_up_b, conv_w, conv_b3, w_down_b, w_ple_b, g_ple3, w_ple_gate_b, i, seq)
    return x2.reshape(batch, seq, d)
```

```python
import functools

import numpy as np
import jax
import jax.numpy as jnp
from jax import lax
from jax.experimental import pallas as pl
from jax.experimental.pallas import tpu as pltpu

F32 = jnp.float32
BF16 = jnp.bfloat16

HEAD_DIM = 64
GRID_W = 64
NA_KH = 8
NA_KW = 16
GQA_GROUP = 4
ROPE_THETA = 10000.0
EPS = 1e-6

LANES = 128
SUBLANES = 8
MXU_DIM = 256
HEADS_PER_SLAB = LANES // HEAD_DIM

NA_ROWS_PER_STEP = 4
NA_KEY_ROWS = 12
MASK_VALUE = -1e30

NA_GROUPS_PER_ITER = 8
LOG2_E = 1.4426950408889634

ROW_TILE = 512
PROJ_ROW_TILE = 1024
PROJ_SUB_ROWS = 256
GQA_Q_TILE = 1024
GQA_K_TILE = 1024
GQA_LOOKAHEAD = 1
FFN_CHUNKS = 1
GQA_BOUND_SLACK = 1.0 + 2.0 ** -6
GQA_MIN_SUM = 2.0 ** -60
VMEM_LIMIT = 56 * 1024 * 1024


def _rms_rows(x, g):
    ms = jnp.mean(x * x, axis=-1, keepdims=True)
    return x * lax.rsqrt(ms + EPS) * g


def _dot(a, b):
    return jnp.dot(a, b, preferred_element_type=F32)


def _dot_nt(a, b):
    return lax.dot_general(a, b, (((1,), (1,)), ((), ())), preferred_element_type=F32)


def _sigmoid(z):
    return 1.0 / (1.0 + jnp.exp2(z * -LOG2_E))


def _layer_spec(a, layer):
    return pl.BlockSpec((None,) + a.shape[1:], lambda *_: (layer, 0, 0), pipeline_mode=pl.Buffered(1))


def _low_half_mask():
    return lax.broadcasted_iota(jnp.int32, (1, LANES), 1) < HEAD_DIM


def _proj_kernel(x_ref, g_ref, w_ref, wvt_ref, hm_ref, gn_ref, cos_ref, sin_ref,
                 qa_ref, ka_ref, qb_ref, kb_ref, va_ref, vt_ref, *, n_norm, n_rope_start):
    even_lane = (lax.broadcasted_iota(jnp.int32, (1, LANES), 1) % 2) == 0
    outs = ((qa_ref, 2), (ka_ref, 2), (qb_ref, 2), (kb_ref, 1), (va_ref, 2))
    slabs = [(o_ref, t) for o_ref, n_slabs in outs for t in range(n_slabs)]
    for r0 in range(0, x_ref.shape[0], PROJ_SUB_ROWS):
        r1 = r0 + PROJ_SUB_ROWS
        h = _rms_rows(x_ref[r0:r1, :], g_ref[...]).astype(BF16)
        ys = [_dot(h, w_ref[:, s * MXU_DIM:(s + 1) * MXU_DIM]) for s in range(len(slabs))]
        vt_ref[:, r0:r1] = _dot_nt(wvt_ref[...], h).astype(vt_ref.dtype)
        mss = [_dot((ys[s] * ys[s]).astype(BF16), hm_ref[...]) for s in range(n_norm)]
        for s, (o_ref, t) in enumerate(slabs):
            y = ys[s]
            if s < n_norm:
                y = y * lax.rsqrt(mss[s] + EPS) * gn_ref[:, s * MXU_DIM:(s + 1) * MXU_DIM]
                if s >= n_rope_start:
                    halves = []
                    for u in range(MXU_DIM // LANES):
                        yh = y[:, u * LANES:(u + 1) * LANES]
                        sw = jnp.where(even_lane, pltpu.roll(yh, LANES - 1, 1), pltpu.roll(yh, 1, 1))
                        halves.append(yh * cos_ref[r0:r1, :] + sw * sin_ref[r0:r1, :])
                    y = jnp.concatenate(halves, axis=1)
            o_ref[r0:r1, t * MXU_DIM:(t + 1) * MXU_DIM] = y.astype(o_ref.dtype)


def _proj_call(x2, g_mix, w_cat, w_vt, head_mean, gains, cos_t, sin_t, layer, batch, seq):
    rows, d = x2.shape
    tm = PROJ_ROW_TILE
    tiles_per_seq = seq // tm
    kv_w = w_vt.shape[1]
    widths = (512, 512, 512, 256, 512)
    row_spec = lambda w: pl.BlockSpec((tm, w), lambda i: (i, 0))
    per_layer = lambda a: _layer_spec(a, layer)
    rope_spec = pl.BlockSpec((tm, LANES), lambda i: (i % tiles_per_seq, 0))
    kern = functools.partial(_proj_kernel, n_norm=7, n_rope_start=4)
    return pl.pallas_call(
        kern,
        grid=(rows // tm,),
        in_specs=[row_spec(d), per_layer(g_mix), per_layer(w_cat), per_layer(w_vt),
                  pl.BlockSpec(head_mean.shape, lambda i: (0, 0), pipeline_mode=pl.Buffered(1)), per_layer(gains),
                  rope_spec, rope_spec],
        out_specs=[row_spec(w) for w in widths]
        + [pl.BlockSpec((None, None, kv_w, tm), lambda i: (i // tiles_per_seq, i % tiles_per_seq, 0, 0))],
        out_shape=[jax.ShapeDtypeStruct((rows, w), BF16) for w in widths]
        + [jax.ShapeDtypeStruct((batch, tiles_per_seq, kv_w, tm), BF16)],
        compiler_params=pltpu.CompilerParams(dimension_semantics=("parallel",), vmem_limit_bytes=VMEM_LIMIT),
        name="proj",
    )(x2, g_mix, w_cat, w_vt, head_mean, gains, cos_t, sin_t)


def _natten_kernel(q_ref, k_ref, v_ref, bias_ref, o_ref, *, n_groups, grid_rows):
    low = _low_half_mask()
    q_rows = NA_ROWS_PER_STEP * GRID_W
    k_rows = NA_KEY_ROWS * GRID_W

    def rows_of(g):
        r0 = g * NA_ROWS_PER_STEP
        k0 = jnp.clip(r0 - NA_KH // 2, 0, grid_rows - NA_KEY_ROWS)
        return pl.multiple_of(r0 * GRID_W, GRID_W), pl.multiple_of(k0 * GRID_W, GRID_W)

    def scores(g, par):
        qs, ks = rows_of(g)
        q2 = q_ref[pl.ds(qs, q_rows), :]
        sel = low if par == 0 else jnp.logical_not(low)
        qm = jnp.where(sel, q2, jnp.zeros_like(q2))
        var = jnp.where(g == 0, 0, jnp.where(g == n_groups - 1, 2, 1))
        return _dot_nt(qm, k_ref[pl.ds(ks, k_rows), :]) + bias_ref[par, var]

    def attend(g, s):
        _, ks = rows_of(g)
        m = jnp.max(s, axis=-1, keepdims=True)
        p = jnp.exp2(s - m)
        l = jnp.sum(p, axis=-1, keepdims=True)
        return _dot(p.astype(BF16), v_ref[pl.ds(ks, k_rows), :]) * (1.0 / l)

    def body(i, carry):
        tasks = [(i * NA_GROUPS_PER_ITER + u, par) for u in range(NA_GROUPS_PER_ITER)
                 for par in range(HEADS_PER_SLAB)]
        s = scores(*tasks[0])
        outs = []
        for t, (g, par) in enumerate(tasks):
            s_next = scores(*tasks[t + 1]) if t + 1 < len(tasks) else None
            outs.append(attend(g, s))
            s = s_next
            if par == HEADS_PER_SLAB - 1:
                qs, _ = rows_of(g)
                o_ref[pl.ds(qs, q_rows), :] = jnp.where(low, outs[0], outs[1]).astype(o_ref.dtype)
                outs = []
        return carry

    lax.fori_loop(0, n_groups // NA_GROUPS_PER_ITER, body, 0)


def _natten_call(qa, ka, va, bias, layer, batch, seq):
    rows, width = qa.shape
    n_slabs = width // LANES
    first = layer * n_slabs
    grid_rows = seq // GRID_W
    n_groups = grid_rows // NA_ROWS_PER_STEP
    slab_spec = pl.BlockSpec((seq, LANES), lambda j, b: (b, j))
    kern = functools.partial(_natten_kernel, n_groups=n_groups, grid_rows=grid_rows)
    return pl.pallas_call(
        kern,
        grid=(n_slabs, batch),
        in_specs=[slab_spec, slab_spec, slab_spec,
                  pl.BlockSpec((HEADS_PER_SLAB,) + bias.shape[1:], lambda j, b: (first + j, 0, 0, 0))],
        out_specs=slab_spec,
        out_shape=jax.ShapeDtypeStruct((rows, width), BF16),
        compiler_params=pltpu.CompilerParams(dimension_semantics=("parallel", "parallel"),
                                             vmem_limit_bytes=VMEM_LIMIT),
        name="natten",
    )(qa, ka, va, bias)


def _natten_bias_kernel(w_ref, o_ref, *, grid_rows):
    n_pairs = w_ref.shape[1]
    c = lax.broadcasted_iota(jnp.int32, (GRID_W, LANES), 0)
    lane = lax.broadcasted_iota(jnp.int32, (GRID_W, LANES), 1)
    kc = lane % GRID_W
    cs = jnp.clip(c - NA_KW // 2, 0, GRID_W - NA_KW)
    col_valid = (kc >= cs) & (kc < cs + NA_KW)
    upper = (lane >= GRID_W).astype(jnp.int32)
    masked = jnp.full((GRID_W, LANES), MASK_VALUE, F32)
    lo, hi = NA_KH // 2 - 1, NA_KH // 2 + NA_KH - 2
    pairs, pairs_interior = [], []
    for a in range(n_pairs):
        x = jnp.broadcast_to(w_ref[0, a:a + 1, :], (GRID_W, LANES))
        y = pltpu.roll(x, LANES - (NA_KW - 1), 1, stride=1, stride_axis=0)
        blk = jnp.where(col_valid, y * LOG2_E, masked)
        row_idx = a + upper
        pairs.append(blk)
        pairs_interior.append(jnp.where((row_idx >= lo) & (row_idx <= hi), blk, masked))
    r0s = (0, NA_ROWS_PER_STEP, grid_rows - NA_ROWS_PER_STEP)
    for var, r0 in enumerate(r0s):
        k0 = min(max(r0 - NA_KH // 2, 0), grid_rows - NA_KEY_ROWS)
        rs = [min(max(r0 + j - NA_KH // 2, 0), grid_rows - NA_KH) for j in range(NA_ROWS_PER_STEP)]
        for j in range(NA_ROWS_PER_STEP):
            for t in range(NA_KEY_ROWS // 2):
                a = 2 * t - j + (k0 - r0) + NA_KH - 1
                rows_ok = [rs[j] <= k0 + rel < rs[j] + NA_KH for rel in (2 * t, 2 * t + 1)]
                if var == 1:
                    blk = pairs_interior[a]
                elif all(rows_ok):
                    blk = pairs[a]
                else:
                    assert not any(rows_ok)
                    blk = masked
                o_ref[0, var, j * GRID_W:(j + 1) * GRID_W, t * LANES:(t + 1) * LANES] = blk


def _natten_bias_call(rpb):
    n_l, n_h, n_r, n_c = rpb.shape
    assert n_r == 2 * NA_KH - 1 and n_c == 2 * NA_KW - 1 and n_c <= GRID_W
    rp = jnp.pad(rpb.astype(F32), ((0, 0), (0, 0), (0, 0), (0, GRID_W - n_c)))
    pairs = jnp.concatenate([rp[:, :, :-1], rp[:, :, 1:]], axis=-1).reshape(n_l * n_h, n_r - 1, LANES)
    q_rows, k_rows = NA_ROWS_PER_STEP * GRID_W, NA_KEY_ROWS * GRID_W
    kern = functools.partial(_natten_bias_kernel, grid_rows=GRID_W)
    return pl.pallas_call(
        kern,
        grid=(n_l * n_h,),
        in_specs=[pl.BlockSpec((1, n_r - 1, LANES), lambda h: (h, 0, 0))],
        out_specs=pl.BlockSpec((1, 3, q_rows, k_rows), lambda h: (h, 0, 0, 0)),
        out_shape=jax.ShapeDtypeStruct((n_l * n_h, 3, q_rows, k_rows), F32),
        compiler_params=pltpu.CompilerParams(dimension_semantics=("parallel",), vmem_limit_bytes=VMEM_LIMIT),
        name="natten_bias",
    )(pairs)


def _gqa_kernel(q_ref, k_ref, vt_ref, o_ref, m_ref, l_ref, acc_ref, kmax_ref):
    low = _low_half_mask()
    tk = GQA_K_TILE
    per_block = vt_ref.shape[2] // tk
    n_chunks = vt_ref.shape[0] * per_block
    n_heads = m_ref.shape[0]
    ones = jnp.ones((SUBLANES, LANES), BF16)

    def row_sq_norms(a):
        af = a.astype(F32)
        return _dot_nt(ones, (af * af).astype(BF16))[0:1]

    @pl.when(pl.program_id(2) == 0)
    def _():
        kmax = jnp.max(row_sq_norms(k_ref[...]), axis=1, keepdims=True) * 0.5
        kmax_ref[...] = jnp.broadcast_to(kmax, kmax_ref.shape)

    def masked_q(hd):
        j, par = divmod(hd, HEADS_PER_SLAB)
        q2 = q_ref[:, j * LANES:(j + 1) * LANES]
        sel = low if par == 0 else jnp.logical_not(low)
        return jnp.where(sel, q2, jnp.zeros_like(q2))

    def scores(c, hd):
        return _dot_nt(k_ref[c * tk:(c + 1) * tk, :], masked_q(hd))

    def values(c):
        b, u = divmod(c, per_block)
        return vt_ref[b, :, u * tk:(u + 1) * tk]

    def update_bounded(c, hd, st):
        pt = jnp.exp2(st - m_ref[hd])
        l_ref[hd] = l_ref[hd] + jnp.sum(pt, axis=0, keepdims=True)
        acc_ref[hd] = acc_ref[hd] + _dot(values(c), pt.astype(BF16))

    def update_online(c, hd, st):
        m_prev = m_ref[hd]
        m_new = jnp.maximum(m_prev, jnp.max(st, axis=0, keepdims=True))
        alpha = jnp.exp2(m_prev - m_new)
        pt = jnp.exp2(st - m_new)
        l_ref[hd] = alpha * l_ref[hd] + jnp.sum(pt, axis=0, keepdims=True)
        acc_ref[hd] = alpha * acc_ref[hd] + _dot(values(c), pt.astype(BF16))
        m_ref[hd] = m_new

    def run(update):
        l_ref[...] = jnp.zeros(l_ref.shape, F32)
        acc_ref[...] = jnp.zeros(acc_ref.shape, F32)
        tasks = [(c, hd) for c in range(n_chunks) for hd in range(n_heads)]
        pending = [scores(*task) for task in tasks[:GQA_LOOKAHEAD]]
        for t, task in enumerate(tasks):
            if t + GQA_LOOKAHEAD < len(tasks):
                pending.append(scores(*tasks[t + GQA_LOOKAHEAD]))
            update(*task, pending.pop(0))

    def store():
        for j in range(n_heads // HEADS_PER_SLAB):
            outs_t = [acc_ref[hd] * (1.0 / l_ref[hd]) for hd in range(j * HEADS_PER_SLAB, (j + 1) * HEADS_PER_SLAB)]
            slab = jnp.concatenate(outs_t, axis=0).T
            o_ref[:, j * LANES:(j + 1) * LANES] = slab.astype(o_ref.dtype)

    for hd in range(n_heads):
        m_ref[hd] = jnp.sqrt(row_sq_norms(masked_q(hd)) * kmax_ref[...]) * GQA_BOUND_SLACK
    run(update_bounded)
    trusted = jnp.min(l_ref[...]) >= GQA_MIN_SUM

    @pl.when(trusted)
    def _():
        store()

    @pl.when(jnp.logical_not(trusted))
    def _():
        m_ref[...] = jnp.full(m_ref.shape, MASK_VALUE, F32)
        run(update_online)
        store()


def _gqa_call(qb, kb, vbt, batch, seq):
    rows, width = qb.shape
    n_kv = kb.shape[1] // LANES
    n_chunks, tk = vbt.shape[1], vbt.shape[3]
    tq = GQA_Q_TILE
    q_width = width // n_kv
    n_heads = q_width // HEAD_DIM
    blocks = seq // tq
    q_spec = pl.BlockSpec((tq, q_width), lambda b, h, i: (b * blocks + i, h))
    k_spec = pl.BlockSpec((seq, LANES), lambda b, h, i: (b, h))
    vt_spec = pl.BlockSpec((None, n_chunks, HEAD_DIM, tk), lambda b, h, i: (b, 0, h, 0))
    return pl.pallas_call(
        _gqa_kernel,
        grid=(batch, n_kv, blocks),
        in_specs=[q_spec, k_spec, vt_spec],
        out_specs=q_spec,
        out_shape=jax.ShapeDtypeStruct((rows, width), BF16),
        scratch_shapes=[pltpu.VMEM((n_heads, 1, tq), F32), pltpu.VMEM((n_heads, 1, tq), F32),
                        pltpu.VMEM((n_heads, HEAD_DIM, tq), F32), pltpu.VMEM((1, tq), F32)],
        compiler_params=pltpu.CompilerParams(dimension_semantics=("parallel", "parallel", "arbitrary"),
                                             vmem_limit_bytes=VMEM_LIMIT),
        name="gqa",
    )(qb, kb, vbt)


def _merge_kernel(x_ref, g_ref, oa_ref, ob_ref, wg_ref, bg_ref, wa_ref, wb_ref, wo_ref, o_ref):
    x = x_ref[...]
    d = x.shape[1]
    h = _rms_rows(x, g_ref[...]).astype(BF16)
    gates = _sigmoid(_dot(h, wg_ref[...]) + bg_ref[...])
    merged = gates[:, :d] * _dot(oa_ref[...], wa_ref[...]) + gates[:, d:] * _dot(ob_ref[...], wb_ref[...])
    o_ref[...] = x + _dot(merged.astype(BF16), wo_ref[...])


def _merge_call(x2, g_mix, out_a, out_b, w_gate, b_gate, w_br_a, w_br_b, w_out, layer):
    rows, d = x2.shape
    tm = ROW_TILE
    row_spec = lambda w: pl.BlockSpec((tm, w), lambda i: (i, 0))
    const = lambda a: _layer_spec(a, layer)
    return pl.pallas_call(
        _merge_kernel,
        grid=(rows // tm,),
        in_specs=[row_spec(d), const(g_mix), row_spec(out_a.shape[1]), row_spec(out_b.shape[1]),
                  const(w_gate), const(b_gate), const(w_br_a), const(w_br_b), const(w_out)],
        out_specs=row_spec(d),
        out_shape=jax.ShapeDtypeStruct((rows, d), F32),
        compiler_params=pltpu.CompilerParams(dimension_semantics=("parallel",), vmem_limit_bytes=VMEM_LIMIT),
        name="merge",
    )(x2, g_mix, out_a, out_b, w_gate, b_gate, w_br_a, w_br_b, w_out)


def _ffn_kernel(x_ref, xp_ref, xn_ref, p_ref, gf_ref, wu_ref, cw_ref, cb_ref, wd_ref, wp_ref, gp_ref, wpg_ref,
                o_ref, ug_ref, uv_ref, *, tiles_per_seq, n_chunks):
    i = pl.program_id(0)
    tm, d = x_ref.shape
    d_ff = wd_ref.shape[0]
    fc = d_ff // n_chunks
    x = x_ref[...]
    gf = gf_ref[...]
    h = _rms_rows(x, gf).astype(BF16)
    prev_ok = ((i % tiles_per_seq) != 0).astype(F32)
    next_ok = (((i + 1) % tiles_per_seq) != 0).astype(F32)
    h_halo = _rms_rows(jnp.concatenate([xp_ref[...], xn_ref[...]], axis=0), gf).astype(BF16)

    def conv_part(u_ref, c0):
        w = wu_ref[:, c0:c0 + fc]
        u = _dot(h, w)
        uh = _dot(h_halo, w)
        u_ref[0:SUBLANES, :] = uh[0:SUBLANES] * prev_ok
        u_ref[SUBLANES:SUBLANES + tm, :] = u
        u_ref[SUBLANES + tm:2 * SUBLANES + tm, :] = uh[SUBLANES:] * next_ok
        cw = cw_ref[:, c0:c0 + fc]
        return (cw[0:1] * u_ref[SUBLANES - 1:SUBLANES - 1 + tm, :] + cw[1:2] * u
                + cw[2:3] * u_ref[SUBLANES + 1:SUBLANES + 1 + tm, :] + cb_ref[:, c0:c0 + fc])

    acc = jnp.zeros((tm, d), F32)
    for c in range(n_chunks):
        yg = conv_part(ug_ref, c * fc)
        yv = conv_part(uv_ref, d_ff + c * fc)
        act = (yg * _sigmoid(yg) * yv).astype(BF16)
        acc = acc + _dot(act, wd_ref[c * fc:(c + 1) * fc, :])
    x2 = x + acc
    e = _dot(p_ref[...].astype(BF16), wp_ref[...])
    gate = _sigmoid(_dot(_rms_rows(x2, gp_ref[...]).astype(BF16), wpg_ref[...]))
    o_ref[...] = x2 + gate * e


def _ffn_call(x2, p2, g_ffn, w_up, conv_w, conv_b, w_down, w_ple, g_ple, w_ple_gate, layer, seq):
    rows, d = x2.shape
    tm = ROW_TILE
    d_ff = w_down.shape[1]
    first_tile = layer * (rows // tm)
    n_chunks = FFN_CHUNKS
    assert d_ff % (n_chunks * LANES) == 0
    fc = d_ff // n_chunks
    tiles_per_seq = seq // tm
    blocks8 = tm // SUBLANES
    last8 = rows // SUBLANES - 1
    row_spec = lambda w: pl.BlockSpec((tm, w), lambda i: (i, 0))
    const = lambda a: _layer_spec(a, layer)
    kern = functools.partial(_ffn_kernel, tiles_per_seq=tiles_per_seq, n_chunks=n_chunks)
    return pl.pallas_call(
        kern,
        grid=(rows // tm,),
        in_specs=[
            row_spec(d),
            pl.BlockSpec((SUBLANES, d), lambda i: (jnp.maximum(i * blocks8 - 1, 0), 0)),
            pl.BlockSpec((SUBLANES, d), lambda i: (jnp.minimum((i + 1) * blocks8, last8), 0)),
            pl.BlockSpec((tm, p2.shape[1]), lambda i: (first_tile + i, 0)),
            const(g_ffn), const(w_up), const(conv_w), const(conv_b), const(w_down),
            const(w_ple), const(g_ple), const(w_ple_gate),
        ],
        out_specs=row_spec(d),
        out_shape=jax.ShapeDtypeStruct((rows, d), F32),
        scratch_shapes=[pltpu.VMEM((tm + 2 * SUBLANES, fc), F32), pltpu.VMEM((tm + 2 * SUBLANES, fc), F32)],
        compiler_params=pltpu.CompilerParams(dimension_semantics=("parallel",), vmem_limit_bytes=VMEM_LIMIT),
        name="ffn",
    )(x2, x2, x2, p2, g_ffn, w_up, conv_w, conv_b, w_down, w_ple, g_ple, w_ple_gate)


def _rope_tables(seq):
    t = np.arange(seq)
    row = (t // GRID_W).astype(np.float32)
    col = (t % GRID_W).astype(np.float32)
    n = HEAD_DIM // 2
    freqs = (np.float32(1.0) / np.float32(ROPE_THETA) ** (np.arange(0, n, 2, dtype=np.float32) / np.float32(n)))
    ang = np.concatenate([row[:, None] * freqs[None, :], col[:, None] * freqs[None, :]], axis=-1)
    cos = np.repeat(np.cos(ang), 2, axis=-1)
    sin = np.repeat(np.sin(ang), 2, axis=-1)
    sign = np.where(np.arange(HEAD_DIM) % 2 == 0, -1.0, 1.0).astype(np.float32)
    reps = LANES // HEAD_DIM
    return (jnp.asarray(np.tile(cos, (1, reps)), dtype=F32),
            jnp.asarray(np.tile(sin * sign[None, :], (1, reps)), dtype=F32))


def _head_mean_matrix():
    idx = jnp.arange(MXU_DIM) // HEAD_DIM
    return jnp.where(idx[:, None] == idx[None, :], 1.0 / HEAD_DIM, 0.0).astype(BF16)


def _dup_heads(w, n_heads):
    lead = w.shape[:-1]
    w = w.reshape(lead + (n_heads, 1, HEAD_DIM))
    return jnp.broadcast_to(w, lead + (n_heads, HEADS_PER_SLAB, HEAD_DIM)).reshape(lead + (n_heads * LANES,))


def kernel(x, p, g_mix, w_in, qn_a, kn_a, rpb, qn_b, kn_b, w_gate, b_gate, w_br_a, w_br_b, w_out, g_ffn, w_up,
           conv_w, conv_b, w_down, w_ple, g_ple, w_ple_gate):
    batch, seq, d = x.shape
    depth = p.shape[0]
    rows = batch * seq
    na_w = w_br_a.shape[1]
    q_w = w_br_b.shape[1]
    n_kv = q_w // HEAD_DIM // GQA_GROUP
    kv_w = n_kv * HEAD_DIM
    assert seq == GRID_W * GRID_W and seq % ROW_TILE == 0 and seq % PROJ_ROW_TILE == 0 and seq % GQA_Q_TILE == 0
    assert w_in.shape[2] == 3 * na_w + q_w + 2 * kv_w

    cos_t, sin_t = _rope_tables(seq)
    head_mean = _head_mean_matrix()
    na_bias = _natten_bias_call(rpb)
    q_scale = HEAD_DIM ** -0.5 * LOG2_E
    x2 = x.reshape(rows, d)

    o1, o2, o3 = na_w, 2 * na_w, 3 * na_w
    o4, o5 = o3 + q_w, o3 + q_w + kv_w
    w_cat = jnp.concatenate(
        [w_in[:, :, :o1], w_in[:, :, o1:o2], w_in[:, :, o3:o4], _dup_heads(w_in[:, :, o4:o5], n_kv),
         w_in[:, :, o2:o3]], axis=2).astype(BF16)
    w_vt = jnp.swapaxes(w_in[:, :, o5:], 1, 2).astype(BF16)
    tile = lambda g, n: jnp.tile(g, (1, n))
    gains = jnp.concatenate(
        [tile(qn_a * q_scale, na_w // HEAD_DIM), tile(kn_a, na_w // HEAD_DIM),
         tile(qn_b * q_scale, q_w // HEAD_DIM), tile(kn_b, n_kv * HEADS_PER_SLAB)], axis=1).astype(F32)
    row3 = lambda a: a[:, None, :]
    bf = lambda a: a.astype(BF16)
    g_mix3, gains3, b_gate3 = row3(g_mix), row3(gains), row3(b_gate)
    w_gate_b, w_br_a_b, w_br_b_b, w_out_b = bf(w_gate), bf(w_br_a), bf(w_br_b), bf(w_out)
    g_ffn3, conv_b3, g_ple3 = row3(g_ffn), row3(conv_b), row3(g_ple)
    w_up_b, w_down_b, w_ple_b, w_ple_gate_b = bf(w_up), bf(w_down), bf(w_ple), bf(w_ple_gate)
    p2 = p.reshape(depth * rows, p.shape[-1])

    for i in range(depth):
        qa, ka, qb, kb, va, vbt = _proj_call(x2, g_mix3, w_cat, w_vt, head_mean, gains3, cos_t, sin_t, i, batch, seq)
        out_a = _natten_call(qa, ka, va, na_bias, i, batch, seq)
        out_b = _gqa_call(qb, kb, vbt, batch, seq)
        x2 = _merge_call(x2, g_mix3, out_a, out_b, w_gate_b, b_gate3, w_br_a_b, w_br_b_b, w_out_b, i)
        x2 = _ffn_call(x2, p2, g_ffn3, w_up_b, conv_w, conv_b3, w_down_b, w_ple_b, g_ple3, w_ple_gate_b, i, seq)
    return x2.reshape(batch, seq, d)
```

```python
import functools

import numpy as np
import jax
import jax.numpy as jnp
from jax import lax
from jax.experimental import pallas as pl
from jax.experimental.pallas import tpu as pltpu

F32 = jnp.float32
BF16 = jnp.bfloat16

HEAD_DIM = 64
GRID_W = 64
NA_KH = 8
NA_KW = 16
GQA_GROUP = 4
ROPE_THETA = 10000.0
EPS = 1e-6

LANES = 128
SUBLANES = 8
MXU_DIM = 256
HEADS_PER_SLAB = LANES // HEAD_DIM

NA_ROWS_PER_STEP = 4
NA_KEY_ROWS = 12
MASK_VALUE = -1e30

NA_GROUPS_PER_ITER = 8
LOG2_E = 1.4426950408889634

ROW_TILE = 512
PROJ_ROW_TILE = 1024
PROJ_SUB_ROWS = 256
GQA_Q_TILE = 1024
GQA_K_TILE = 1024
GQA_LOOKAHEAD = 1
FFN_CHUNKS = 1
GQA_BOUND_SLACK = 1.0 + 2.0 ** -6
GQA_MIN_SUM = 2.0 ** -60
VMEM_LIMIT = 56 * 1024 * 1024


def _rms_rows(x, g):
    ms = jnp.mean(x * x, axis=-1, keepdims=True)
    return x * lax.rsqrt(ms + EPS) * g


def _dot(a, b):
    return jnp.dot(a, b, preferred_element_type=F32)


def _dot_nt(a, b):
    return lax.dot_general(a, b, (((1,), (1,)), ((), ())), preferred_element_type=F32)


def _sigmoid(z):
    return 1.0 / (1.0 + jnp.exp2(z * -LOG2_E))


def _layer_spec(a, layer):
    return pl.BlockSpec((None,) + a.shape[1:], lambda *_: (layer, 0, 0), pipeline_mode=pl.Buffered(1))


def _low_half_mask():
    return lax.broadcasted_iota(jnp.int32, (1, LANES), 1) < HEAD_DIM


def _proj_kernel(x_ref, g_ref, w_ref, wvt_ref, hm_ref, gn_ref, cos_ref, sin_ref,
                 qa_ref, ka_ref, qb_ref, kb_ref, va_ref, vt_ref, *, na_w, q_w, kv_w):
    lane = lax.broadcasted_iota(jnp.int32, (1, LANES), 1)
    even_lane = (lane % 2) == 0
    low = lane < HEAD_DIM
    assert kv_w == LANES
    slabs = []
    for o_ref, src, width, gain_col, rope in ((qa_ref, 0, na_w, 0, False), (ka_ref, na_w, na_w, na_w, False),
                                              (qb_ref, 3 * na_w, q_w, 2 * na_w, True),
                                              (kb_ref, 3 * na_w + q_w, kv_w, 2 * na_w + q_w, True),
                                              (va_ref, 2 * na_w, na_w, None, False)):
        for c in range(0, width, MXU_DIM):
            w = min(MXU_DIM, width - c)
            slabs.append((o_ref, c, src + c, w, None if gain_col is None else gain_col + c, rope))
    for r0 in range(0, x_ref.shape[0], PROJ_SUB_ROWS):
        r1 = r0 + PROJ_SUB_ROWS
        h = _rms_rows(x_ref[r0:r1, :], g_ref[...]).astype(BF16)
        ys = [_dot(h, w_ref[:, src:src + w]) for _, _, src, w, _, _ in slabs]
        vt_ref[:, r0:r1] = _dot_nt(wvt_ref[...], h).astype(vt_ref.dtype)
        mss = [None if gc is None else _dot((y * y).astype(BF16), hm_ref[0:w, 0:w])
               for y, (_, _, _, w, gc, _) in zip(ys, slabs)]
        for y, ms, (o_ref, oc, _, w, gc, rope) in zip(ys, mss, slabs):
            if gc is not None:
                y = y * lax.rsqrt(ms + EPS) * gn_ref[:, gc:gc + w]
            if rope:
                parts = []
                for u in range(w // LANES):
                    yh = y[:, u * LANES:(u + 1) * LANES]
                    sw = jnp.where(even_lane, pltpu.roll(yh, LANES - 1, 1), pltpu.roll(yh, 1, 1))
                    parts.append(yh * cos_ref[r0:r1, :] + sw * sin_ref[r0:r1, :])
                y = parts[0] if len(parts) == 1 else jnp.concatenate(parts, axis=1)
            if o_ref is kb_ref:
                swapped = pltpu.roll(y, HEAD_DIM, 1)
                o_ref[r0:r1, 0:LANES] = jnp.where(low, y, swapped).astype(o_ref.dtype)
                o_ref[r0:r1, LANES:2 * LANES] = jnp.where(low, swapped, y).astype(o_ref.dtype)
            else:
                o_ref[r0:r1, oc:oc + w] = y.astype(o_ref.dtype)


def _proj_call(x2, g_mix, w_in, w_vt, head_mean, gains, cos_t, sin_t, na_w, q_w, layer, batch, seq):
    rows, d = x2.shape
    tm = PROJ_ROW_TILE
    tiles_per_seq = seq // tm
    kv_w = w_vt.shape[1]
    widths = (na_w, na_w, q_w, kv_w * HEADS_PER_SLAB, na_w)
    row_spec = lambda w: pl.BlockSpec((tm, w), lambda i: (i, 0))
    per_layer = lambda a: _layer_spec(a, layer)
    rope_spec = pl.BlockSpec((tm, LANES), lambda i: (i % tiles_per_seq, 0))
    kern = functools.partial(_proj_kernel, na_w=na_w, q_w=q_w, kv_w=kv_w)
    return pl.pallas_call(
        kern,
        grid=(rows // tm,),
        in_specs=[row_spec(d), per_layer(g_mix), per_layer(w_in), per_layer(w_vt),
                  pl.BlockSpec(head_mean.shape, lambda i: (0, 0), pipeline_mode=pl.Buffered(1)), per_layer(gains),
                  rope_spec, rope_spec],
        out_specs=[row_spec(w) for w in widths]
        + [pl.BlockSpec((None, None, kv_w, tm), lambda i: (i // tiles_per_seq, i % tiles_per_seq, 0, 0))],
        out_shape=[jax.ShapeDtypeStruct((rows, w), BF16) for w in widths]
        + [jax.ShapeDtypeStruct((batch, tiles_per_seq, kv_w, tm), BF16)],
        compiler_params=pltpu.CompilerParams(dimension_semantics=("parallel",), vmem_limit_bytes=VMEM_LIMIT),
        name="proj",
    )(x2, g_mix, w_in, w_vt, head_mean, gains, cos_t, sin_t)


def _natten_kernel(q_ref, k_ref, v_ref, bias_ref, o_ref, *, n_groups, grid_rows):
    low = _low_half_mask()
    q_rows = NA_ROWS_PER_STEP * GRID_W
    k_rows = NA_KEY_ROWS * GRID_W

    def rows_of(g):
        r0 = g * NA_ROWS_PER_STEP
        k0 = jnp.clip(r0 - NA_KH // 2, 0, grid_rows - NA_KEY_ROWS)
        return pl.multiple_of(r0 * GRID_W, GRID_W), pl.multiple_of(k0 * GRID_W, GRID_W)

    def scores(g, par):
        qs, ks = rows_of(g)
        q2 = q_ref[pl.ds(qs, q_rows), :]
        sel = low if par == 0 else jnp.logical_not(low)
        qm = jnp.where(sel, q2, jnp.zeros_like(q2))
        var = jnp.where(g == 0, 0, jnp.where(g == n_groups - 1, 2, 1))
        return _dot_nt(qm, k_ref[pl.ds(ks, k_rows), :]) + bias_ref[par, var]

    def attend(g, s):
        _, ks = rows_of(g)
        m = jnp.max(s, axis=-1, keepdims=True)
        p = jnp.exp2(s - m)
        l = jnp.sum(p, axis=-1, keepdims=True)
        return _dot(p.astype(BF16), v_ref[pl.ds(ks, k_rows), :]) * (1.0 / l)

    def body(i, carry):
        tasks = [(i * NA_GROUPS_PER_ITER + u, par) for u in range(NA_GROUPS_PER_ITER)
                 for par in range(HEADS_PER_SLAB)]
        s = scores(*tasks[0])
        outs = []
        for t, (g, par) in enumerate(tasks):
            s_next = scores(*tasks[t + 1]) if t + 1 < len(tasks) else None
            outs.append(attend(g, s))
            s = s_next
            if par == HEADS_PER_SLAB - 1:
                qs, _ = rows_of(g)
                o_ref[pl.ds(qs, q_rows), :] = jnp.where(low, outs[0], outs[1]).astype(o_ref.dtype)
                outs = []
        return carry

    lax.fori_loop(0, n_groups // NA_GROUPS_PER_ITER, body, 0)


def _natten_call(qa, ka, va, bias, layer, batch, seq):
    rows, width = qa.shape
    n_slabs = width // LANES
    first = layer * n_slabs
    grid_rows = seq // GRID_W
    n_groups = grid_rows // NA_ROWS_PER_STEP
    slab_spec = pl.BlockSpec((seq, LANES), lambda j, b: (b, j))
    kern = functools.partial(_natten_kernel, n_groups=n_groups, grid_rows=grid_rows)
    return pl.pallas_call(
        kern,
        grid=(n_slabs, batch),
        in_specs=[slab_spec, slab_spec, slab_spec,
                  pl.BlockSpec((HEADS_PER_SLAB,) + bias.shape[1:], lambda j, b: (first + j, 0, 0, 0))],
        out_specs=slab_spec,
        out_shape=jax.ShapeDtypeStruct((rows, width), BF16),
        compiler_params=pltpu.CompilerParams(dimension_semantics=("parallel", "parallel"),
                                             vmem_limit_bytes=VMEM_LIMIT),
        name="natten",
    )(qa, ka, va, bias)


def _natten_bias_kernel(w_ref, o_ref, *, grid_rows):
    n_pairs = w_ref.shape[1]
    c = lax.broadcasted_iota(jnp.int32, (GRID_W, LANES), 0)
    lane = lax.broadcasted_iota(jnp.int32, (GRID_W, LANES), 1)
    kc = lane % GRID_W
    cs = jnp.clip(c - NA_KW // 2, 0, GRID_W - NA_KW)
    col_valid = (kc >= cs) & (kc < cs + NA_KW)
    upper = (lane >= GRID_W).astype(jnp.int32)
    masked = jnp.full((GRID_W, LANES), MASK_VALUE, F32)
    lo, hi = NA_KH // 2 - 1, NA_KH // 2 + NA_KH - 2
    pairs, pairs_interior = [], []
    for a in range(n_pairs):
        x = jnp.broadcast_to(w_ref[0, a:a + 1, :], (GRID_W, LANES))
        y = pltpu.roll(x, LANES - (NA_KW - 1), 1, stride=1, stride_axis=0)
        blk = jnp.where(col_valid, y * LOG2_E, masked)
        row_idx = a + upper
        pairs.append(blk)
        pairs_interior.append(jnp.where((row_idx >= lo) & (row_idx <= hi), blk, masked))
    r0s = (0, NA_ROWS_PER_STEP, grid_rows - NA_ROWS_PER_STEP)
    for var, r0 in enumerate(r0s):
        k0 = min(max(r0 - NA_KH // 2, 0), grid_rows - NA_KEY_ROWS)
        rs = [min(max(r0 + j - NA_KH // 2, 0), grid_rows - NA_KH) for j in range(NA_ROWS_PER_STEP)]
        for j in range(NA_ROWS_PER_STEP):
            for t in range(NA_KEY_ROWS // 2):
                a = 2 * t - j + (k0 - r0) + NA_KH - 1
                rows_ok = [rs[j] <= k0 + rel < rs[j] + NA_KH for rel in (2 * t, 2 * t + 1)]
                if var == 1:
                    blk = pairs_interior[a]
                elif all(rows_ok):
                    blk = pairs[a]
                else:
                    assert not any(rows_ok)
                    blk = masked
                o_ref[0, var, j * GRID_W:(j + 1) * GRID_W, t * LANES:(t + 1) * LANES] = blk


def _natten_bias_call(rpb):
    n_l, n_h, n_r, n_c = rpb.shape
    assert n_r == 2 * NA_KH - 1 and n_c == 2 * NA_KW - 1 and n_c <= GRID_W
    rp = jnp.pad(rpb.astype(F32), ((0, 0), (0, 0), (0, 0), (0, GRID_W - n_c)))
    pairs = jnp.concatenate([rp[:, :, :-1], rp[:, :, 1:]], axis=-1).reshape(n_l * n_h, n_r - 1, LANES)
    q_rows, k_rows = NA_ROWS_PER_STEP * GRID_W, NA_KEY_ROWS * GRID_W
    kern = functools.partial(_natten_bias_kernel, grid_rows=GRID_W)
    return pl.pallas_call(
        kern,
        grid=(n_l * n_h,),
        in_specs=[pl.BlockSpec((1, n_r - 1, LANES), lambda h: (h, 0, 0))],
        out_specs=pl.BlockSpec((1, 3, q_rows, k_rows), lambda h: (h, 0, 0, 0)),
        out_shape=jax.ShapeDtypeStruct((n_l * n_h, 3, q_rows, k_rows), F32),
        compiler_params=pltpu.CompilerParams(dimension_semantics=("parallel",), vmem_limit_bytes=VMEM_LIMIT),
        name="natten_bias",
    )(pairs)


def _gqa_kernel(q_ref, k_ref, vt_ref, o_ref, m_ref, l_ref, acc_ref, kmax_ref):
    low = _low_half_mask()
    tk = GQA_K_TILE
    per_block = vt_ref.shape[2] // tk
    n_chunks = vt_ref.shape[0] * per_block
    n_heads = m_ref.shape[0]
    ones = jnp.ones((SUBLANES, LANES), BF16)

    def row_sq_norms(a):
        af = a.astype(F32)
        return _dot_nt(ones, (af * af).astype(BF16))[0:1]

    @pl.when(pl.program_id(2) == 0)
    def _():
        kmax = jnp.max(row_sq_norms(k_ref[...]), axis=1, keepdims=True) * 0.5
        kmax_ref[...] = jnp.broadcast_to(kmax, kmax_ref.shape)

    def masked_q(hd):
        j, par = divmod(hd, HEADS_PER_SLAB)
        q2 = q_ref[:, j * LANES:(j + 1) * LANES]
        sel = low if par == 0 else jnp.logical_not(low)
        return jnp.where(sel, q2, jnp.zeros_like(q2))

    def scores(c, hd):
        return _dot_nt(k_ref[c * tk:(c + 1) * tk, :], masked_q(hd))

    def values(c):
        b, u = divmod(c, per_block)
        return vt_ref[b, :, u * tk:(u + 1) * tk]

    def update_bounded(c, hd, st):
        pt = jnp.exp2(st - m_ref[hd])
        l_ref[hd] = l_ref[hd] + jnp.sum(pt, axis=0, keepdims=True)
        acc_ref[hd] = acc_ref[hd] + _dot(values(c), pt.astype(BF16))

    def update_online(c, hd, st):
        m_prev = m_ref[hd]
        m_new = jnp.maximum(m_prev, jnp.max(st, axis=0, keepdims=True))
        alpha = jnp.exp2(m_prev - m_new)
        pt = jnp.exp2(st - m_new)
        l_ref[hd] = alpha * l_ref[hd] + jnp.sum(pt, axis=0, keepdims=True)
        acc_ref[hd] = alpha * acc_ref[hd] + _dot(values(c), pt.astype(BF16))
        m_ref[hd] = m_new

    def run(update):
        l_ref[...] = jnp.zeros(l_ref.shape, F32)
        acc_ref[...] = jnp.zeros(acc_ref.shape, F32)
        tasks = [(c, hd) for c in range(n_chunks) for hd in range(n_heads)]
        pending = [scores(*task) for task in tasks[:GQA_LOOKAHEAD]]
        for t, task in enumerate(tasks):
            if t + GQA_LOOKAHEAD < len(tasks):
                pending.append(scores(*tasks[t + GQA_LOOKAHEAD]))
            update(*task, pending.pop(0))

    def store():
        for j in range(n_heads // HEADS_PER_SLAB):
            outs_t = [acc_ref[hd] * (1.0 / l_ref[hd]) for hd in range(j * HEADS_PER_SLAB, (j + 1) * HEADS_PER_SLAB)]
            slab = jnp.concatenate(outs_t, axis=0).T
            o_ref[:, j * LANES:(j + 1) * LANES] = slab.astype(o_ref.dtype)

    for hd in range(n_heads):
        m_ref[hd] = jnp.sqrt(row_sq_norms(masked_q(hd)) * kmax_ref[...]) * GQA_BOUND_SLACK
    run(update_bounded)
    trusted = jnp.min(l_ref[...]) >= GQA_MIN_SUM

    @pl.when(trusted)
    def _():
        store()

    @pl.when(jnp.logical_not(trusted))
    def _():
        m_ref[...] = jnp.full(m_ref.shape, MASK_VALUE, F32)
        run(update_online)
        store()


def _gqa_call(qb, kb, vbt, batch, seq):
    rows, width = qb.shape
    n_kv = kb.shape[1] // LANES
    n_chunks, tk = vbt.shape[1], vbt.shape[3]
    tq = GQA_Q_TILE
    q_width = width // n_kv
    n_heads = q_width // HEAD_DIM
    blocks = seq // tq
    q_spec = pl.BlockSpec((tq, q_width), lambda b, h, i: (b * blocks + i, h))
    k_spec = pl.BlockSpec((seq, LANES), lambda b, h, i: (b, h))
    vt_spec = pl.BlockSpec((None, n_chunks, HEAD_DIM, tk), lambda b, h, i: (b, 0, h, 0))
    return pl.pallas_call(
        _gqa_kernel,
        grid=(batch, n_kv, blocks),
        in_specs=[q_spec, k_spec, vt_spec],
        out_specs=q_spec,
        out_shape=jax.ShapeDtypeStruct((rows, width), BF16),
        scratch_shapes=[pltpu.VMEM((n_heads, 1, tq), F32), pltpu.VMEM((n_heads, 1, tq), F32),
                        pltpu.VMEM((n_heads, HEAD_DIM, tq), F32), pltpu.VMEM((1, tq), F32)],
        compiler_params=pltpu.CompilerParams(dimension_semantics=("parallel", "parallel", "arbitrary"),
                                             vmem_limit_bytes=VMEM_LIMIT),
        name="gqa",
    )(qb, kb, vbt)


def _merge_kernel(x_ref, g_ref, oa_ref, ob_ref, wg_ref, bg_ref, wa_ref, wb_ref, wo_ref, o_ref):
    x = x_ref[...]
    d = x.shape[1]
    h = _rms_rows(x, g_ref[...]).astype(BF16)
    gates = _sigmoid(_dot(h, wg_ref[...]) + bg_ref[...])
    merged = gates[:, :d] * _dot(oa_ref[...], wa_ref[...]) + gates[:, d:] * _dot(ob_ref[...], wb_ref[...])
    o_ref[...] = x + _dot(merged.astype(BF16), wo_ref[...])


def _merge_call(x2, g_mix, out_a, out_b, w_gate, b_gate, w_br_a, w_br_b, w_out, layer):
    rows, d = x2.shape
    tm = ROW_TILE
    row_spec = lambda w: pl.BlockSpec((tm, w), lambda i: (i, 0))
    const = lambda a: _layer_spec(a, layer)
    return pl.pallas_call(
        _merge_kernel,
        grid=(rows // tm,),
        in_specs=[row_spec(d), const(g_mix), row_spec(out_a.shape[1]), row_spec(out_b.shape[1]),
                  const(w_gate), const(b_gate), const(w_br_a), const(w_br_b), const(w_out)],
        out_specs=row_spec(d),
        out_shape=jax.ShapeDtypeStruct((rows, d), F32),
        compiler_params=pltpu.CompilerParams(dimension_semantics=("parallel",), vmem_limit_bytes=VMEM_LIMIT),
        name="merge",
    )(x2, g_mix, out_a, out_b, w_gate, b_gate, w_br_a, w_br_b, w_out)


def _ffn_kernel(x_ref, xp_ref, xn_ref, p_ref, gf_ref, wu_ref, cw_ref, cb_ref, wd_ref, wp_ref, gp_ref, wpg_ref,
                o_ref, ug_ref, uv_ref, *, tiles_per_seq, n_chunks):
    i = pl.program_id(0)
    tm, d = x_ref.shape
    d_ff = wd_ref.shape[0]
    fc = d_ff // n_chunks
    x = x_ref[...]
    gf = gf_ref[...]
    h = _rms_rows(x, gf).astype(BF16)
    prev_ok = ((i % tiles_per_seq) != 0).astype(F32)
    next_ok = (((i + 1) % tiles_per_seq) != 0).astype(F32)
    h_halo = _rms_rows(jnp.concatenate([xp_ref[...], xn_ref[...]], axis=0), gf).astype(BF16)

    def conv_part(u_ref, c0):
        w = wu_ref[:, c0:c0 + fc]
        u = _dot(h, w)
        uh = _dot(h_halo, w)
        u_ref[0:SUBLANES, :] = uh[0:SUBLANES] * prev_ok
        u_ref[SUBLANES:SUBLANES + tm, :] = u
        u_ref[SUBLANES + tm:2 * SUBLANES + tm, :] = uh[SUBLANES:] * next_ok
        cw = cw_ref[:, c0:c0 + fc]
        return (cw[0:1] * u_ref[SUBLANES - 1:SUBLANES - 1 + tm, :] + cw[1:2] * u
                + cw[2:3] * u_ref[SUBLANES + 1:SUBLANES + 1 + tm, :] + cb_ref[:, c0:c0 + fc])

    acc = jnp.zeros((tm, d), F32)
    for c in range(n_chunks):
        yg = conv_part(ug_ref, c * fc)
        yv = conv_part(uv_ref, d_ff + c * fc)
        act = (yg * _sigmoid(yg) * yv).astype(BF16)
        acc = acc + _dot(act, wd_ref[c * fc:(c + 1) * fc, :])
    x2 = x + acc
    e = _dot(p_ref[...].astype(BF16), wp_ref[...])
    gate = _sigmoid(_dot(_rms_rows(x2, gp_ref[...]).astype(BF16), wpg_ref[...]))
    o_ref[...] = x2 + gate * e


def _ffn_call(x2, p2, g_ffn, w_up, conv_w, conv_b, w_down, w_ple, g_ple, w_ple_gate, layer, seq):
    rows, d = x2.shape
    tm = ROW_TILE
    d_ff = w_down.shape[1]
    first_tile = layer * (rows // tm)
    n_chunks = FFN_CHUNKS
    assert d_ff % (n_chunks * LANES) == 0
    fc = d_ff // n_chunks
    tiles_per_seq = seq // tm
    blocks8 = tm // SUBLANES
    last8 = rows // SUBLANES - 1
    row_spec = lambda w: pl.BlockSpec((tm, w), lambda i: (i, 0))
    const = lambda a: _layer_spec(a, layer)
    kern = functools.partial(_ffn_kernel, tiles_per_seq=tiles_per_seq, n_chunks=n_chunks)
    return pl.pallas_call(
        kern,
        grid=(rows // tm,),
        in_specs=[
            row_spec(d),
            pl.BlockSpec((SUBLANES, d), lambda i: (jnp.maximum(i * blocks8 - 1, 0), 0)),
            pl.BlockSpec((SUBLANES, d), lambda i: (jnp.minimum((i + 1) * blocks8, last8), 0)),
            pl.BlockSpec((tm, p2.shape[1]), lambda i: (first_tile + i, 0)),
            const(g_ffn), const(w_up), const(conv_w), const(conv_b), const(w_down),
            const(w_ple), const(g_ple), const(w_ple_gate),
        ],
        out_specs=row_spec(d),
        out_shape=jax.ShapeDtypeStruct((rows, d), F32),
        scratch_shapes=[pltpu.VMEM((tm + 2 * SUBLANES, fc), F32), pltpu.VMEM((tm + 2 * SUBLANES, fc), F32)],
        compiler_params=pltpu.CompilerParams(dimension_semantics=("parallel",), vmem_limit_bytes=VMEM_LIMIT),
        name="ffn",
    )(x2, x2, x2, p2, g_ffn, w_up, conv_w, conv_b, w_down, w_ple, g_ple, w_ple_gate)


def _rope_tables(seq):
    t = np.arange(seq)
    row = (t // GRID_W).astype(np.float32)
    col = (t % GRID_W).astype(np.float32)
    n = HEAD_DIM // 2
    freqs = (np.float32(1.0) / np.float32(ROPE_THETA) ** (np.arange(0, n, 2, dtype=np.float32) / np.float32(n)))
    ang = np.concatenate([row[:, None] * freqs[None, :], col[:, None] * freqs[None, :]], axis=-1)
    cos = np.repeat(np.cos(ang), 2, axis=-1)
    sin = np.repeat(np.sin(ang), 2, axis=-1)
    sign = np.where(np.arange(HEAD_DIM) % 2 == 0, -1.0, 1.0).astype(np.float32)
    reps = LANES // HEAD_DIM
    return (jnp.asarray(np.tile(cos, (1, reps)), dtype=F32),
            jnp.asarray(np.tile(sin * sign[None, :], (1, reps)), dtype=F32))


def _head_mean_matrix():
    idx = jnp.arange(MXU_DIM) // HEAD_DIM
    return jnp.where(idx[:, None] == idx[None, :], 1.0 / HEAD_DIM, 0.0).astype(BF16)


def kernel(x, p, g_mix, w_in, qn_a, kn_a, rpb, qn_b, kn_b, w_gate, b_gate, w_br_a, w_br_b, w_out, g_ffn, w_up,
           conv_w, conv_b, w_down, w_ple, g_ple, w_ple_gate):
    batch, seq, d = x.shape
    depth = p.shape[0]
    rows = batch * seq
    na_w = w_br_a.shape[1]
    q_w = w_br_b.shape[1]
    n_kv = q_w // HEAD_DIM // GQA_GROUP
    kv_w = n_kv * HEAD_DIM
    assert seq == GRID_W * GRID_W and seq % ROW_TILE == 0 and seq % PROJ_ROW_TILE == 0 and seq % GQA_Q_TILE == 0
    assert w_in.shape[2] == 3 * na_w + q_w + 2 * kv_w

    cos_t, sin_t = _rope_tables(seq)
    head_mean = _head_mean_matrix()
    na_bias = _natten_bias_call(rpb)
    q_scale = HEAD_DIM ** -0.5 * LOG2_E
    x2 = x.reshape(rows, d)

    w_vt = jnp.swapaxes(w_in[:, :, 3 * na_w + q_w + kv_w:], 1, 2).astype(BF16)
    tile = lambda g, n: jnp.tile(g, (1, n))
    gains = jnp.concatenate(
        [tile(qn_a * q_scale, na_w // HEAD_DIM), tile(kn_a, na_w // HEAD_DIM),
         tile(qn_b * q_scale, q_w // HEAD_DIM), tile(kn_b, n_kv)], axis=1).astype(F32)
    row3 = lambda a: a[:, None, :]
    bf = lambda a: a.astype(BF16)
    w_in_b = bf(w_in)
    g_mix3, gains3, b_gate3 = row3(g_mix), row3(gains), row3(b_gate)
    w_gate_b, w_br_a_b, w_br_b_b, w_out_b = bf(w_gate), bf(w_br_a), bf(w_br_b), bf(w_out)
    g_ffn3, conv_b3, g_ple3 = row3(g_ffn), row3(conv_b), row3(g_ple)
    w_up_b, w_down_b, w_ple_b, w_ple_gate_b = bf(w_up), bf(w_down), bf(w_ple), bf(w_ple_gate)
    p2 = p.reshape(depth * rows, p.shape[-1])

    for i in range(depth):
        qa, ka, qb, kb, va, vbt = _proj_call(x2, g_mix3, w_in_b, w_vt, head_mean, gains3, cos_t, sin_t, na_w, q_w,
                                             i, batch, seq)
        out_a = _natten_call(qa, ka, va, na_bias, i, batch, seq)
        out_b = _gqa_call(qb, kb, vbt, batch, seq)
        x2 = _merge_call(x2, g_mix3, out_a, out_b, w_gate_b, b_gate3, w_br_a_b, w_br_b_b, w_out_b, i)
        x2 = _ffn_call(x2, p2, g_ffn3, w_up_b, conv_w, conv_b3, w_down_b, w_ple_b, g_ple3, w_ple_gate_b, i, seq)
    return x2.reshape(batch, seq, d)
```

```python
import functools

import numpy as np
import jax
import jax.numpy as jnp
from jax import lax
from jax.experimental import pallas as pl
from jax.experimental.pallas import tpu as pltpu

F32 = jnp.float32
BF16 = jnp.bfloat16

HEAD_DIM = 64
GRID_W = 64
NA_KH = 8
NA_KW = 16
GQA_GROUP = 4
ROPE_THETA = 10000.0
EPS = 1e-6

LANES = 128
SUBLANES = 8
MXU_DIM = 256
HEADS_PER_SLAB = LANES // HEAD_DIM

NA_ROWS_PER_STEP = 4
NA_KEY_ROWS = 12
MASK_VALUE = -1e30

NA_GROUPS_PER_ITER = 8
LOG2_E = 1.4426950408889634

ROW_TILE = 512
MERGE_ROW_TILE = 1024
PROJ_ROW_TILE = 1024
PROJ_SUB_ROWS = 256
GQA_Q_TILE = 1024
GQA_K_TILE = 1024
GQA_LOOKAHEAD = 1
FFN_CHUNKS = 1
GQA_BOUND_SLACK = 1.0 + 2.0 ** -6
GQA_MIN_SUM = 2.0 ** -60
VMEM_LIMIT = 56 * 1024 * 1024


def _rms_rows(x, g):
    ms = jnp.mean(x * x, axis=-1, keepdims=True)
    return x * lax.rsqrt(ms + EPS) * g


def _dot(a, b):
    return jnp.dot(a, b, preferred_element_type=F32)


def _dot_nt(a, b):
    return lax.dot_general(a, b, (((1,), (1,)), ((), ())), preferred_element_type=F32)


def _sigmoid(z):
    return 1.0 / (1.0 + jnp.exp2(z * -LOG2_E))


def _layer_spec(a, layer):
    return pl.BlockSpec((None,) + a.shape[1:], lambda *_: (layer, 0, 0), pipeline_mode=pl.Buffered(1))


def _low_half_mask():
    return lax.broadcasted_iota(jnp.int32, (1, LANES), 1) < HEAD_DIM


def _proj_kernel(x_ref, g_ref, w_ref, hm_ref, gn_ref, cos_ref, sin_ref,
                 qa_ref, ka_ref, qb_ref, kb_ref, va_ref, vt_ref, *, na_w, q_w, kv_w):
    lane = lax.broadcasted_iota(jnp.int32, (1, LANES), 1)
    even_lane = (lane % 2) == 0
    low = lane < HEAD_DIM
    assert kv_w == LANES
    slabs = []
    for o_ref, src, width, gain_col, rope in ((qa_ref, 0, na_w, 0, False), (ka_ref, na_w, na_w, na_w, False),
                                              (qb_ref, 3 * na_w, q_w, 2 * na_w, True),
                                              (kb_ref, 3 * na_w + q_w, kv_w, 2 * na_w + q_w, True),
                                              (va_ref, 2 * na_w, na_w, None, False),
                                              (vt_ref, 3 * na_w + q_w + kv_w, kv_w, None, False)):
        for c in range(0, width, MXU_DIM):
            w = min(MXU_DIM, width - c)
            slabs.append((o_ref, c, src + c, w, None if gain_col is None else gain_col + c, rope))
    for r0 in range(0, x_ref.shape[0], PROJ_SUB_ROWS):
        r1 = r0 + PROJ_SUB_ROWS
        h = _rms_rows(x_ref[r0:r1, :], g_ref[...]).astype(BF16)
        ys = [_dot(h, w_ref[:, src:src + w]) for _, _, src, w, _, _ in slabs]
        mss = [None if gc is None else _dot((y * y).astype(BF16), hm_ref[0:w, 0:w])
               for y, (_, _, _, w, gc, _) in zip(ys, slabs)]
        for y, ms, (o_ref, oc, _, w, gc, rope) in zip(ys, mss, slabs):
            if gc is not None:
                y = y * lax.rsqrt(ms + EPS) * gn_ref[:, gc:gc + w]
            if rope:
                parts = []
                for u in range(w // LANES):
                    yh = y[:, u * LANES:(u + 1) * LANES]
                    sw = jnp.where(even_lane, pltpu.roll(yh, LANES - 1, 1), pltpu.roll(yh, 1, 1))
                    parts.append(yh * cos_ref[r0:r1, :] + sw * sin_ref[r0:r1, :])
                y = parts[0] if len(parts) == 1 else jnp.concatenate(parts, axis=1)
            if o_ref is kb_ref:
                swapped = pltpu.roll(y, HEAD_DIM, 1)
                o_ref[r0:r1, 0:LANES] = jnp.where(low, y, swapped).astype(o_ref.dtype)
                o_ref[r0:r1, LANES:2 * LANES] = jnp.where(low, swapped, y).astype(o_ref.dtype)
            elif o_ref is vt_ref:
                o_ref[:, r0:r1] = y.T.astype(o_ref.dtype)
            else:
                o_ref[r0:r1, oc:oc + w] = y.astype(o_ref.dtype)


def _proj_call(x2, g_mix, w_in, head_mean, gains, cos_t, sin_t, na_w, q_w, layer, batch, seq):
    rows, d = x2.shape
    tm = PROJ_ROW_TILE
    tiles_per_seq = seq // tm
    kv_w = (w_in.shape[2] - 3 * na_w - q_w) // 2
    widths = (na_w, na_w, q_w, kv_w * HEADS_PER_SLAB, na_w)
    row_spec = lambda w: pl.BlockSpec((tm, w), lambda i: (i, 0))
    per_layer = lambda a: _layer_spec(a, layer)
    rope_spec = pl.BlockSpec((tm, LANES), lambda i: (i % tiles_per_seq, 0))
    kern = functools.partial(_proj_kernel, na_w=na_w, q_w=q_w, kv_w=kv_w)
    return pl.pallas_call(
        kern,
        grid=(rows // tm,),
        in_specs=[row_spec(d), per_layer(g_mix), per_layer(w_in),
                  pl.BlockSpec(head_mean.shape, lambda i: (0, 0), pipeline_mode=pl.Buffered(1)), per_layer(gains),
                  rope_spec, rope_spec],
        out_specs=[row_spec(w) for w in widths]
        + [pl.BlockSpec((None, None, kv_w, tm), lambda i: (i // tiles_per_seq, i % tiles_per_seq, 0, 0))],
        out_shape=[jax.ShapeDtypeStruct((rows, w), BF16) for w in widths]
        + [jax.ShapeDtypeStruct((batch, tiles_per_seq, kv_w, tm), BF16)],
        compiler_params=pltpu.CompilerParams(dimension_semantics=("parallel",), vmem_limit_bytes=VMEM_LIMIT),
        name="proj",
    )(x2, g_mix, w_in, head_mean, gains, cos_t, sin_t)


def _natten_kernel(q_ref, k_ref, v_ref, bias_ref, o_ref, *, n_groups, grid_rows):
    low = _low_half_mask()
    q_rows = NA_ROWS_PER_STEP * GRID_W
    k_rows = NA_KEY_ROWS * GRID_W

    def rows_of(g):
        r0 = g * NA_ROWS_PER_STEP
        k0 = jnp.clip(r0 - NA_KH // 2, 0, grid_rows - NA_KEY_ROWS)
        return pl.multiple_of(r0 * GRID_W, GRID_W), pl.multiple_of(k0 * GRID_W, GRID_W)

    def scores(g, par):
        qs, ks = rows_of(g)
        q2 = q_ref[pl.ds(qs, q_rows), :]
        sel = low if par == 0 else jnp.logical_not(low)
        qm = jnp.where(sel, q2, jnp.zeros_like(q2))
        var = jnp.where(g == 0, 0, jnp.where(g == n_groups - 1, 2, 1))
        return _dot_nt(qm, k_ref[pl.ds(ks, k_rows), :]) + bias_ref[par, var]

    def attend(g, s):
        _, ks = rows_of(g)
        m = jnp.max(s, axis=-1, keepdims=True)
        p = jnp.exp2(s - m)
        l = jnp.sum(p, axis=-1, keepdims=True)
        return _dot(p.astype(BF16), v_ref[pl.ds(ks, k_rows), :]) * (1.0 / l)

    def body(i, carry):
        tasks = [(i * NA_GROUPS_PER_ITER + u, par) for u in range(NA_GROUPS_PER_ITER)
                 for par in range(HEADS_PER_SLAB)]
        s = scores(*tasks[0])
        outs = []
        for t, (g, par) in enumerate(tasks):
            s_next = scores(*tasks[t + 1]) if t + 1 < len(tasks) else None
            outs.append(attend(g, s))
            s = s_next
            if par == HEADS_PER_SLAB - 1:
                qs, _ = rows_of(g)
                o_ref[pl.ds(qs, q_rows), :] = jnp.where(low, outs[0], outs[1]).astype(o_ref.dtype)
                outs = []
        return carry

    lax.fori_loop(0, n_groups // NA_GROUPS_PER_ITER, body, 0)


def _natten_call(qa, ka, va, bias, layer, batch, seq):
    rows, width = qa.shape
    n_slabs = width // LANES
    first = layer * n_slabs
    grid_rows = seq // GRID_W
    n_groups = grid_rows // NA_ROWS_PER_STEP
    slab_spec = pl.BlockSpec((seq, LANES), lambda j, b: (b, j))
    kern = functools.partial(_natten_kernel, n_groups=n_groups, grid_rows=grid_rows)
    return pl.pallas_call(
        kern,
        grid=(n_slabs, batch),
        in_specs=[slab_spec, slab_spec, slab_spec,
                  pl.BlockSpec((HEADS_PER_SLAB,) + bias.shape[1:], lambda j, b: (first + j, 0, 0, 0))],
        out_specs=slab_spec,
        out_shape=jax.ShapeDtypeStruct((rows, width), BF16),
        compiler_params=pltpu.CompilerParams(dimension_semantics=("parallel", "parallel"),
                                             vmem_limit_bytes=VMEM_LIMIT),
        name="natten",
    )(qa, ka, va, bias)


def _natten_bias_kernel(w_ref, o_ref, *, grid_rows):
    n_pairs = w_ref.shape[1]
    c = lax.broadcasted_iota(jnp.int32, (GRID_W, LANES), 0)
    lane = lax.broadcasted_iota(jnp.int32, (GRID_W, LANES), 1)
    kc = lane % GRID_W
    cs = jnp.clip(c - NA_KW // 2, 0, GRID_W - NA_KW)
    col_valid = (kc >= cs) & (kc < cs + NA_KW)
    upper = (lane >= GRID_W).astype(jnp.int32)
    masked = jnp.full((GRID_W, LANES), MASK_VALUE, F32)
    lo, hi = NA_KH // 2 - 1, NA_KH // 2 + NA_KH - 2
    pairs, pairs_interior = [], []
    for a in range(n_pairs):
        x = jnp.broadcast_to(w_ref[0, a:a + 1, :], (GRID_W, LANES))
        y = pltpu.roll(x, LANES - (NA_KW - 1), 1, stride=1, stride_axis=0)
        blk = jnp.where(col_valid, y * LOG2_E, masked)
        row_idx = a + upper
        pairs.append(blk)
        pairs_interior.append(jnp.where((row_idx >= lo) & (row_idx <= hi), blk, masked))
    r0s = (0, NA_ROWS_PER_STEP, grid_rows - NA_ROWS_PER_STEP)
    for var, r0 in enumerate(r0s):
        k0 = min(max(r0 - NA_KH // 2, 0), grid_rows - NA_KEY_ROWS)
        rs = [min(max(r0 + j - NA_KH // 2, 0), grid_rows - NA_KH) for j in range(NA_ROWS_PER_STEP)]
        for j in range(NA_ROWS_PER_STEP):
            for t in range(NA_KEY_ROWS // 2):
                a = 2 * t - j + (k0 - r0) + NA_KH - 1
                rows_ok = [rs[j] <= k0 + rel < rs[j] + NA_KH for rel in (2 * t, 2 * t + 1)]
                if var == 1:
                    blk = pairs_interior[a]
                elif all(rows_ok):
                    blk = pairs[a]
                else:
                    assert not any(rows_ok)
                    blk = masked
                o_ref[0, var, j * GRID_W:(j + 1) * GRID_W, t * LANES:(t + 1) * LANES] = blk


def _natten_bias_call(rpb):
    n_l, n_h, n_r, n_c = rpb.shape
    assert n_r == 2 * NA_KH - 1 and n_c == 2 * NA_KW - 1 and n_c <= GRID_W
    rp = jnp.pad(rpb.astype(F32), ((0, 0), (0, 0), (0, 0), (0, GRID_W - n_c)))
    pairs = jnp.concatenate([rp[:, :, :-1], rp[:, :, 1:]], axis=-1).reshape(n_l * n_h, n_r - 1, LANES)
    q_rows, k_rows = NA_ROWS_PER_STEP * GRID_W, NA_KEY_ROWS * GRID_W
    kern = functools.partial(_natten_bias_kernel, grid_rows=GRID_W)
    return pl.pallas_call(
        kern,
        grid=(n_l * n_h,),
        in_specs=[pl.BlockSpec((1, n_r - 1, LANES), lambda h: (h, 0, 0))],
        out_specs=pl.BlockSpec((1, 3, q_rows, k_rows), lambda h: (h, 0, 0, 0)),
        out_shape=jax.ShapeDtypeStruct((n_l * n_h, 3, q_rows, k_rows), F32),
        compiler_params=pltpu.CompilerParams(dimension_semantics=("parallel",), vmem_limit_bytes=VMEM_LIMIT),
        name="natten_bias",
    )(pairs)


def _gqa_kernel(q_ref, k_ref, vt_ref, o_ref, m_ref, l_ref, acc_ref, kmax_ref):
    low = _low_half_mask()
    tk = GQA_K_TILE
    per_block = vt_ref.shape[2] // tk
    n_chunks = vt_ref.shape[0] * per_block
    n_heads = m_ref.shape[0]
    ones = jnp.ones((SUBLANES, LANES), BF16)

    def row_sq_norms(a):
        af = a.astype(F32)
        return _dot_nt(ones, (af * af).astype(BF16))[0:1]

    @pl.when(pl.program_id(2) == 0)
    def _():
        kmax = jnp.max(row_sq_norms(k_ref[...]), axis=1, keepdims=True) * 0.5
        kmax_ref[...] = jnp.broadcast_to(kmax, kmax_ref.shape)

    def masked_q(hd):
        j, par = divmod(hd, HEADS_PER_SLAB)
        q2 = q_ref[:, j * LANES:(j + 1) * LANES]
        sel = low if par == 0 else jnp.logical_not(low)
        return jnp.where(sel, q2, jnp.zeros_like(q2))

    def scores(c, hd):
        return _dot_nt(k_ref[c * tk:(c + 1) * tk, :], masked_q(hd))

    def values(c):
        b, u = divmod(c, per_block)
        return vt_ref[b, :, u * tk:(u + 1) * tk]

    def update_bounded(c, hd, st):
        pt = jnp.exp2(st - m_ref[hd])
        l_ref[hd] = l_ref[hd] + jnp.sum(pt, axis=0, keepdims=True)
        acc_ref[hd] = acc_ref[hd] + _dot(values(c), pt.astype(BF16))

    def update_online(c, hd, st):
        m_prev = m_ref[hd]
        m_new = jnp.maximum(m_prev, jnp.max(st, axis=0, keepdims=True))
        alpha = jnp.exp2(m_prev - m_new)
        pt = jnp.exp2(st - m_new)
        l_ref[hd] = alpha * l_ref[hd] + jnp.sum(pt, axis=0, keepdims=True)
        acc_ref[hd] = alpha * acc_ref[hd] + _dot(values(c), pt.astype(BF16))
        m_ref[hd] = m_new

    def run(update):
        l_ref[...] = jnp.zeros(l_ref.shape, F32)
        acc_ref[...] = jnp.zeros(acc_ref.shape, F32)
        tasks = [(c, hd) for c in range(n_chunks) for hd in range(n_heads)]
        pending = [scores(*task) for task in tasks[:GQA_LOOKAHEAD]]
        for t, task in enumerate(tasks):
            if t + GQA_LOOKAHEAD < len(tasks):
                pending.append(scores(*tasks[t + GQA_LOOKAHEAD]))
            update(*task, pending.pop(0))

    def store():
        for j in range(n_heads // HEADS_PER_SLAB):
            outs_t = [acc_ref[hd] * (1.0 / l_ref[hd]) for hd in range(j * HEADS_PER_SLAB, (j + 1) * HEADS_PER_SLAB)]
            slab = jnp.concatenate(outs_t, axis=0).T
            o_ref[:, j * LANES:(j + 1) * LANES] = slab.astype(o_ref.dtype)

    for hd in range(n_heads):
        m_ref[hd] = jnp.sqrt(row_sq_norms(masked_q(hd)) * kmax_ref[...]) * GQA_BOUND_SLACK
    run(update_bounded)
    trusted = jnp.min(l_ref[...]) >= GQA_MIN_SUM

    @pl.when(trusted)
    def _():
        store()

    @pl.when(jnp.logical_not(trusted))
    def _():
        m_ref[...] = jnp.full(m_ref.shape, MASK_VALUE, F32)
        run(update_online)
        store()


def _gqa_call(qb, kb, vbt, batch, seq):
    rows, width = qb.shape
    n_kv = kb.shape[1] // LANES
    n_chunks, tk = vbt.shape[1], vbt.shape[3]
    tq = GQA_Q_TILE
    q_width = width // n_kv
    n_heads = q_width // HEAD_DIM
    blocks = seq // tq
    q_spec = pl.BlockSpec((tq, q_width), lambda b, h, i: (b * blocks + i, h))
    k_spec = pl.BlockSpec((seq, LANES), lambda b, h, i: (b, h))
    vt_spec = pl.BlockSpec((None, n_chunks, HEAD_DIM, tk), lambda b, h, i: (b, 0, h, 0))
    return pl.pallas_call(
        _gqa_kernel,
        grid=(batch, n_kv, blocks),
        in_specs=[q_spec, k_spec, vt_spec],
        out_specs=q_spec,
        out_shape=jax.ShapeDtypeStruct((rows, width), BF16),
        scratch_shapes=[pltpu.VMEM((n_heads, 1, tq), F32), pltpu.VMEM((n_heads, 1, tq), F32),
                        pltpu.VMEM((n_heads, HEAD_DIM, tq), F32), pltpu.VMEM((1, tq), F32)],
        compiler_params=pltpu.CompilerParams(dimension_semantics=("parallel", "parallel", "arbitrary"),
                                             vmem_limit_bytes=VMEM_LIMIT),
        name="gqa",
    )(qb, kb, vbt)


def _merge_kernel(x_ref, g_ref, oa_ref, ob_ref, wg_ref, bg_ref, wa_ref, wb_ref, wo_ref, o_ref):
    x = x_ref[...]
    d = x.shape[1]
    h = _rms_rows(x, g_ref[...]).astype(BF16)
    gates = _sigmoid(_dot(h, wg_ref[...]) + bg_ref[...])
    merged = gates[:, :d] * _dot(oa_ref[...], wa_ref[...]) + gates[:, d:] * _dot(ob_ref[...], wb_ref[...])
    o_ref[...] = x + _dot(merged.astype(BF16), wo_ref[...])


def _merge_call(x2, g_mix, out_a, out_b, w_gate, b_gate, w_br_a, w_br_b, w_out, layer):
    rows, d = x2.shape
    tm = MERGE_ROW_TILE
    row_spec = lambda w: pl.BlockSpec((tm, w), lambda i: (i, 0))
    const = lambda a: _layer_spec(a, layer)
    return pl.pallas_call(
        _merge_kernel,
        grid=(rows // tm,),
        in_specs=[row_spec(d), const(g_mix), row_spec(out_a.shape[1]), row_spec(out_b.shape[1]),
                  const(w_gate), const(b_gate), const(w_br_a), const(w_br_b), const(w_out)],
        out_specs=row_spec(d),
        out_shape=jax.ShapeDtypeStruct((rows, d), F32),
        compiler_params=pltpu.CompilerParams(dimension_semantics=("parallel",), vmem_limit_bytes=VMEM_LIMIT),
        name="merge",
    )(x2, g_mix, out_a, out_b, w_gate, b_gate, w_br_a, w_br_b, w_out)


def _ffn_kernel(x_ref, xp_ref, xn_ref, p_ref, gf_ref, wu_ref, cw_ref, cb_ref, wd_ref, wp_ref, gp_ref, wpg_ref,
                o_ref, ug_ref, uv_ref, *, tiles_per_seq, n_chunks):
    i = pl.program_id(0)
    tm, d = x_ref.shape
    d_ff = wd_ref.shape[0]
    fc = d_ff // n_chunks
    x = x_ref[...]
    gf = gf_ref[...]
    h = _rms_rows(x, gf).astype(BF16)
    prev_ok = ((i % tiles_per_seq) != 0).astype(F32)
    next_ok = (((i + 1) % tiles_per_seq) != 0).astype(F32)
    h_halo = _rms_rows(jnp.concatenate([xp_ref[...], xn_ref[...]], axis=0), gf).astype(BF16)

    def conv_part(u_ref, c0):
        w = wu_ref[:, c0:c0 + fc]
        u = _dot(h, w)
        uh = _dot(h_halo, w)
        u_ref[0:SUBLANES, :] = uh[0:SUBLANES] * prev_ok
        u_ref[SUBLANES:SUBLANES + tm, :] = u
        u_ref[SUBLANES + tm:2 * SUBLANES + tm, :] = uh[SUBLANES:] * next_ok
        cw = cw_ref[:, c0:c0 + fc]
        return (cw[0:1] * u_ref[SUBLANES - 1:SUBLANES - 1 + tm, :] + cw[1:2] * u
                + cw[2:3] * u_ref[SUBLANES + 1:SUBLANES + 1 + tm, :] + cb_ref[:, c0:c0 + fc])

    acc = jnp.zeros((tm, d), F32)
    for c in range(n_chunks):
        yg = conv_part(ug_ref, c * fc)
        yv = conv_part(uv_ref, d_ff + c * fc)
        act = (yg * _sigmoid(yg) * yv).astype(BF16)
        acc = acc + _dot(act, wd_ref[c * fc:(c + 1) * fc, :])
    x2 = x + acc
    e = _dot(p_ref[...].astype(BF16), wp_ref[...])
    gate = _sigmoid(_dot(_rms_rows(x2, gp_ref[...]).astype(BF16), wpg_ref[...]))
    o_ref[...] = x2 + gate * e


def _ffn_call(x2, p2, g_ffn, w_up, conv_w, conv_b, w_down, w_ple, g_ple, w_ple_gate, layer, seq):
    rows, d = x2.shape
    tm = ROW_TILE
    d_ff = w_down.shape[1]
    first_tile = layer * (rows // tm)
    n_chunks = FFN_CHUNKS
    assert d_ff % (n_chunks * LANES) == 0
    fc = d_ff // n_chunks
    tiles_per_seq = seq // tm
    blocks8 = tm // SUBLANES
    last8 = rows // SUBLANES - 1
    row_spec = lambda w: pl.BlockSpec((tm, w), lambda i: (i, 0))
    const = lambda a: _layer_spec(a, layer)
    kern = functools.partial(_ffn_kernel, tiles_per_seq=tiles_per_seq, n_chunks=n_chunks)
    return pl.pallas_call(
        kern,
        grid=(rows // tm,),
        in_specs=[
            row_spec(d),
            pl.BlockSpec((SUBLANES, d), lambda i: (jnp.maximum(i * blocks8 - 1, 0), 0)),
            pl.BlockSpec((SUBLANES, d), lambda i: (jnp.minimum((i + 1) * blocks8, last8), 0)),
            pl.BlockSpec((tm, p2.shape[1]), lambda i: (first_tile + i, 0)),
            const(g_ffn), const(w_up), const(conv_w), const(conv_b), const(w_down),
            const(w_ple), const(g_ple), const(w_ple_gate),
        ],
        out_specs=row_spec(d),
        out_shape=jax.ShapeDtypeStruct((rows, d), F32),
        scratch_shapes=[pltpu.VMEM((tm + 2 * SUBLANES, fc), F32), pltpu.VMEM((tm + 2 * SUBLANES, fc), F32)],
        compiler_params=pltpu.CompilerParams(dimension_semantics=("parallel",), vmem_limit_bytes=VMEM_LIMIT),
        name="ffn",
    )(x2, x2, x2, p2, g_ffn, w_up, conv_w, conv_b, w_down, w_ple, g_ple, w_ple_gate)


def _rope_tables(seq):
    t = np.arange(seq)
    row = (t // GRID_W).astype(np.float32)
    col = (t % GRID_W).astype(np.float32)
    n = HEAD_DIM // 2
    freqs = (np.float32(1.0) / np.float32(ROPE_THETA) ** (np.arange(0, n, 2, dtype=np.float32) / np.float32(n)))
    ang = np.concatenate([row[:, None] * freqs[None, :], col[:, None] * freqs[None, :]], axis=-1)
    cos = np.repeat(np.cos(ang), 2, axis=-1)
    sin = np.repeat(np.sin(ang), 2, axis=-1)
    sign = np.where(np.arange(HEAD_DIM) % 2 == 0, -1.0, 1.0).astype(np.float32)
    reps = LANES // HEAD_DIM
    return (jnp.asarray(np.tile(cos, (1, reps)), dtype=F32),
            jnp.asarray(np.tile(sin * sign[None, :], (1, reps)), dtype=F32))


def _head_mean_matrix():
    idx = jnp.arange(MXU_DIM) // HEAD_DIM
    return jnp.where(idx[:, None] == idx[None, :], 1.0 / HEAD_DIM, 0.0).astype(BF16)


def kernel(x, p, g_mix, w_in, qn_a, kn_a, rpb, qn_b, kn_b, w_gate, b_gate, w_br_a, w_br_b, w_out, g_ffn, w_up,
           conv_w, conv_b, w_down, w_ple, g_ple, w_ple_gate):
    batch, seq, d = x.shape
    depth = p.shape[0]
    rows = batch * seq
    na_w = w_br_a.shape[1]
    q_w = w_br_b.shape[1]
    n_kv = q_w // HEAD_DIM // GQA_GROUP
    kv_w = n_kv * HEAD_DIM
    assert seq == GRID_W * GRID_W and seq % ROW_TILE == 0 and seq % PROJ_ROW_TILE == 0 and seq % GQA_Q_TILE == 0
    assert w_in.shape[2] == 3 * na_w + q_w + 2 * kv_w

    cos_t, sin_t = _rope_tables(seq)
    head_mean = _head_mean_matrix()
    na_bias = _natten_bias_call(rpb)
    q_scale = HEAD_DIM ** -0.5 * LOG2_E
    x2 = x.reshape(rows, d)

    tile = lambda g, n: jnp.tile(g, (1, n))
    gains = jnp.concatenate(
        [tile(qn_a * q_scale, na_w // HEAD_DIM), tile(kn_a, na_w // HEAD_DIM),
         tile(qn_b * q_scale, q_w // HEAD_DIM), tile(kn_b, n_kv)], axis=1).astype(F32)
    row3 = lambda a: a[:, None, :]
    bf = lambda a: a.astype(BF16)
    w_in_b = bf(w_in)
    g_mix3, gains3, b_gate3 = row3(g_mix), row3(gains), row3(b_gate)
    w_gate_b, w_br_a_b, w_br_b_b, w_out_b = bf(w_gate), bf(w_br_a), bf(w_br_b), bf(w_out)
    g_ffn3, conv_b3, g_ple3 = row3(g_ffn), row3(conv_b), row3(g_ple)
    w_up_b, w_down_b, w_ple_b, w_ple_gate_b = bf(w_up), bf(w_down), bf(w_ple), bf(w_ple_gate)
    p2 = p.reshape(depth * rows, p.shape[-1])

    for i in range(depth):
        qa, ka, qb, kb, va, vbt = _proj_call(x2, g_mix3, w_in_b, head_mean, gains3, cos_t, sin_t, na_w, q_w,
                                             i, batch, seq)
        out_a = _natten_call(qa, ka, va, na_bias, i, batch, seq)
        out_b = _gqa_call(qb, kb, vbt, batch, seq)
        x2 = _merge_call(x2, g_mix3, out_a, out_b, w_gate_b, b_gate3, w_br_a_b, w_br_b_b, w_out_b, i)
        x2 = _ffn_call(x2, p2, g_ffn3, w_up_b, conv_w, conv_b3, w_down_b, w_ple_b, g_ple3, w_ple_gate_b, i, seq)
    return x2.reshape(batch, seq, d)
```
